```python
import math
import jax, jax.numpy as jnp
from jax import lax
import numpy as np

D_MODEL = 1024
BATCH = 4
SEQ = 4096
DEPTH = 2

N_EVEN = (DEPTH + 1) // 2
N_ODD = DEPTH // 2
ALPHA = (2.0 * DEPTH) ** 0.25
BETA = (8.0 * DEPTH) ** -0.25
LN_EPS = 1e-5

CONV_DIM = D_MODEL // 2
CONV_WIDTH = 31
SGU_DIM = D_MODEL // 2
SGU_GROUPS = 4
SGU_GROUP_DIM = SGU_DIM // SGU_GROUPS
CHUNK = 128
AB_IN = 2 * CONV_DIM + 2 * SGU_DIM
AB_MIX = CONV_DIM + SGU_DIM

RWKV_HEAD = 64
RWKV_HEADS = D_MODEL // RWKV_HEAD
DECAY_LORA = 64
AAA_LORA = 64
GATE_LORA = 128
GN_EPS = 64e-5

FFN_HIDDEN = -(-8 * D_MODEL // (3 * 256)) * 256

kernel_name = 'hybrid_conv_sgu_rwkv7_deepnorm'


def _layer_norm(x, g, b, eps=LN_EPS):
    xf = x.astype(jnp.float32)
    mu = jnp.mean(xf, axis=-1, keepdims=True)
    var = jnp.mean(jnp.square(xf - mu), axis=-1, keepdims=True)
    return ((xf - mu) * lax.rsqrt(var + eps) * g + b).astype(x.dtype)


def conv_sgu_mixer(x, w_in, b_in, conv_w, conv_b, cn_g, cn_b, sn_g, sn_b, sgu_w, sgu_b, w_out, b_out):
    bsz, s, _ = x.shape
    h = jnp.einsum('bsd,de->bse', x, w_in) + b_in
    a_val, a_gate, u, v = jnp.split(h, [CONV_DIM, 2 * CONV_DIM, 2 * CONV_DIM + SGU_DIM], axis=-1)
    y = a_val * jax.nn.sigmoid(a_gate)
    y = lax.conv_general_dilated(
        y, conv_w, window_strides=(1,), padding=((CONV_WIDTH - 1, 0),),
        dimension_numbers=('NWC', 'WIO', 'NWC'), feature_group_count=CONV_DIM) + conv_b
    y = jax.nn.silu(_layer_norm(y, cn_g, cn_b))
    u = jax.nn.gelu(u)
    v = _layer_norm(jax.nn.gelu(v), sn_g, sn_b)
    v = v.reshape(bsz, s // CHUNK, CHUNK, SGU_GROUPS, SGU_GROUP_DIM)
    causal = jnp.tril(jnp.ones((CHUNK, CHUNK), dtype=bool))
    w_s = jnp.where(causal[None], sgu_w, jnp.zeros((), sgu_w.dtype))
    sv = jnp.einsum('gts,bcsgd->bctgd', w_s, v) + jnp.transpose(sgu_b)[:, :, None]
    z = u * sv.reshape(bsz, s, SGU_DIM)
    mixed = jnp.concatenate([y, z], axis=-1)
    return jnp.einsum('bse,ed->bsd', mixed, w_out) + b_out


def rwkv7_time_mix(x, mu, w_rkv, w0, w_w1, w_w2, a0, a_w1, a_w2, g_w1, g_w2,
                   k_k, k_a, r_k, ln_g, ln_b, w_out):
    bsz, s, d = x.shape
    x_prev = jnp.pad(x, ((0, 0), (1, 0), (0, 0)))[:, :-1]
    xx = x_prev - x
    xr, xw, xk, xv, xa, xg = [x + xx * mu[i] for i in range(6)]
    r, k, v = jnp.einsum('nbsd,nde->nbse', jnp.stack([xr, xk, xv]), w_rkv)
    w_log = -jax.nn.softplus(-(w0 + jnp.tanh(xw @ w_w1) @ w_w2)) - 0.5
    decay = jnp.exp(-jnp.exp(w_log.astype(jnp.float32)))
    a = jax.nn.sigmoid(a0 + (xa @ a_w1) @ a_w2)
    g = jax.nn.sigmoid(xg @ g_w1) @ g_w2
    hs = (bsz, s, RWKV_HEADS, RWKV_HEAD)
    kk = (k * k_k).reshape(hs).astype(jnp.float32)
    kk = kk / jnp.maximum(jnp.linalg.norm(kk, axis=-1, keepdims=True), 1e-12)
    k = k * (1 + (a - 1) * k_a)
    rf = r.reshape(hs).astype(jnp.float32)
    kf = k.reshape(hs).astype(jnp.float32)
    vf = v.reshape(hs).astype(jnp.float32)
    af = a.reshape(hs).astype(jnp.float32)
    wf = decay.reshape(hs)

    def step(state, inp):
        r_t, w_t, k_t, v_t, kk_t, a_t = inp
        sa = jnp.einsum('bhij,bhj->bhi', state, -kk_t)
        state = (state * w_t[:, :, None, :]
                 + sa[..., None] * (kk_t * a_t)[:, :, None, :]
                 + v_t[..., None] * k_t[:, :, None, :])
        return state, jnp.einsum('bhij,bhj->bhi', state, r_t)

    seq_first = lambda t: jnp.swapaxes(t, 0, 1)
    state0 = jnp.zeros((bsz, RWKV_HEADS, RWKV_HEAD, RWKV_HEAD), jnp.float32)
    _, y = lax.scan(step, state0, tuple(seq_first(t) for t in (rf, wf, kf, vf, kk, af)))
    y = jnp.swapaxes(y, 0, 1)
    ym = jnp.mean(y, axis=-1, keepdims=True)
    yv = jnp.mean(jnp.square(y - ym), axis=-1, keepdims=True)
    y = ((y - ym) * lax.rsqrt(yv + GN_EPS)).reshape(bsz, s, d) * ln_g + ln_b
    bonus = jnp.sum(rf * kf * r_k, axis=-1, keepdims=True) * vf
    y = y + bonus.reshape(bsz, s, d)
    return ((y * g) @ w_out).astype(x.dtype)


def swiglu_ffn(x, w_in, w_out):
    gate, up = jnp.split(x @ w_in, 2, axis=-1)
    return (jax.nn.silu(gate) * up) @ w_out


def setup_inputs(seed: int = 0) -> dict:
    key = jax.random.key(seed)
    ks = iter(jax.random.split(key, 48))
    nrm = lambda shape, scale: scale * jax.random.normal(next(ks), shape, jnp.float32)
    gain = lambda shape: 1.0 + nrm(shape, 0.02)
    D = D_MODEL
    inp = {}
    inp['x'] = nrm((BATCH, SEQ, D), 1.0)
    inp['ab_w_in'] = nrm((N_EVEN, D, AB_IN), D ** -0.5)
    inp['ab_b_in'] = nrm((N_EVEN, AB_IN), 0.02)
    inp['conv_w'] = nrm((N_EVEN, CONV_WIDTH, 1, CONV_DIM), CONV_WIDTH ** -0.5)
    inp['conv_b'] = nrm((N_EVEN, CONV_DIM), 0.02)
    inp['conv_norm_g'] = gain((N_EVEN, CONV_DIM))
    inp['conv_norm_b'] = nrm((N_EVEN, CONV_DIM), 0.02)
    inp['sgu_norm_g'] = gain((N_EVEN, SGU_DIM))
    inp['sgu_norm_b'] = nrm((N_EVEN, SGU_DIM), 0.02)
    inp['sgu_w'] = nrm((N_EVEN, SGU_GROUPS, CHUNK, CHUNK), CHUNK ** -0.5)
    inp['sgu_b'] = 1.0 + nrm((N_EVEN, SGU_GROUPS, CHUNK), 0.1)
    inp['ab_w_out'] = nrm((N_EVEN, AB_MIX, D), BETA * AB_MIX ** -0.5)
    inp['ab_b_out'] = nrm((N_EVEN, D), 0.02)
    inp['rwkv_mu'] = jax.random.uniform(next(ks), (N_ODD, 6, D), jnp.float32)
    inp['rwkv_w_rkv'] = nrm((N_ODD, 3, D, D), D ** -0.5)
    inp['rwkv_w0'] = jax.random.uniform(next(ks), (N_ODD, D), jnp.float32, -6.0, 0.0)
    inp['rwkv_w_w1'] = nrm((N_ODD, D, DECAY_LORA), D ** -0.5)
    inp['rwkv_w_w2'] = nrm((N_ODD, DECAY_LORA, D), 0.1 * DECAY_LORA ** -0.5)
    inp['rwkv_a0'] = nrm((N_ODD, D), 0.1)
    inp['rwkv_a_w1'] = nrm((N_ODD, D, AAA_LORA), D ** -0.5)
    inp['rwkv_a_w2'] = nrm((N_ODD, AAA_LORA, D), 0.1 * AAA_LORA ** -0.5)
    inp['rwkv_g_w1'] = nrm((N_ODD, D, GATE_LORA), D ** -0.5)
    inp['rwkv_g_w2'] = nrm((N_ODD, GATE_LORA, D), GATE_LORA ** -0.5)
    inp['rwkv_k_k'] = 0.85 + nrm((N_ODD, D), 0.02)
    inp['rwkv_k_a'] = gain((N_ODD, D))
    inp['rwkv_r_k'] = nrm((N_ODD, RWKV_HEADS, RWKV_HEAD), 0.1)
    inp['rwkv_ln_g'] = gain((N_ODD, D))
    inp['rwkv_ln_b'] = nrm((N_ODD, D), 0.02)
    inp['rwkv_w_out'] = nrm((N_ODD, D, D), BETA * D ** -0.5)
    inp['ffn_w_in'] = nrm((DEPTH, D, 2 * FFN_HIDDEN), D ** -0.5)
    inp['ffn_w_out'] = nrm((DEPTH, FFN_HIDDEN, D), BETA * FFN_HIDDEN ** -0.5)
    inp['ln_mix_g'] = gain((DEPTH, D))
    inp['ln_mix_b'] = nrm((DEPTH, D), 0.02)
    inp['ln_ffn_g'] = gain((DEPTH, D))
    inp['ln_ffn_b'] = nrm((DEPTH, D), 0.02)
    return inp


def reference(x, ab_w_in, ab_b_in, conv_w, conv_b, conv_norm_g, conv_norm_b,
              sgu_norm_g, sgu_norm_b, sgu_w, sgu_b, ab_w_out, ab_b_out,
              rwkv_mu, rwkv_w_rkv, rwkv_w0, rwkv_w_w1, rwkv_w_w2, rwkv_a0,
              rwkv_a_w1, rwkv_a_w2, rwkv_g_w1, rwkv_g_w2, rwkv_k_k, rwkv_k_a,
              rwkv_r_k, rwkv_ln_g, rwkv_ln_b, rwkv_w_out,
              ffn_w_in, ffn_w_out, ln_mix_g, ln_mix_b, ln_ffn_g, ln_ffn_b):
    for layer in range(DEPTH):
        i = layer // 2
        if layer % 2 == 0:
            mix = conv_sgu_mixer(x, ab_w_in[i], ab_b_in[i], conv_w[i], conv_b[i],
                                 conv_norm_g[i], conv_norm_b[i], sgu_norm_g[i], sgu_norm_b[i],
                                 sgu_w[i], sgu_b[i], ab_w_out[i], ab_b_out[i])
        else:
            mix = rwkv7_time_mix(x, rwkv_mu[i], rwkv_w_rkv[i], rwkv_w0[i], rwkv_w_w1[i],
                                 rwkv_w_w2[i], rwkv_a0[i], rwkv_a_w1[i], rwkv_a_w2[i],
                                 rwkv_g_w1[i], rwkv_g_w2[i], rwkv_k_k[i], rwkv_k_a[i],
                                 rwkv_r_k[i], rwkv_ln_g[i], rwkv_ln_b[i], rwkv_w_out[i])
        x = _layer_norm(ALPHA * x + mix, ln_mix_g[layer], ln_mix_b[layer])
        x = _layer_norm(ALPHA * x + swiglu_ffn(x, ffn_w_in[layer], ffn_w_out[layer]),
                        ln_ffn_g[layer], ln_ffn_b[layer])
    return x
```

```python
import functools
import math

import jax
import jax.numpy as jnp
from jax import lax
from jax.experimental import pallas as pl
from jax.experimental.pallas import tpu as pltpu

F32 = jnp.float32
BF16 = jnp.bfloat16

D_MODEL = 1024
DEPTH = 2
ALPHA = (2.0 * DEPTH) ** 0.25
LN_EPS = 1e-5

CONV_DIM = 512
CONV_WIDTH = 31
SGU_DIM = 512
SGU_GROUPS = 4
SGU_GROUP_DIM = SGU_DIM // SGU_GROUPS
SGU_CHUNK = 128

HEAD = 64
GN_EPS = 64e-5
FFN_HIDDEN = 2816

LANES = 128
MXU_DIM = 256
VMEM_LIMIT_BYTES = 56 * 1024 * 1024

MIX0_ROWS = 512
CONV_HALO = 32
CONV_ROW_BLOCK = 32
FFN_ROWS = 512
FFN_COLS = 256
RWKV_ROWS = 256
RWKV_CHUNK = 64
QUAD = MXU_DIM
HEADS_PER_QUAD = QUAD // HEAD


def _dot(a, b):
    return jnp.dot(a, b, preferred_element_type=F32)


def _dot_nt(a, b):
    return lax.dot_general(a, b, (((1,), (1,)), ((), ())), preferred_element_type=F32)


def _layer_norm(x, g, b, eps=LN_EPS):
    mu = jnp.mean(x, axis=-1, keepdims=True)
    xc = x - mu
    var = jnp.mean(xc * xc, axis=-1, keepdims=True)
    return xc * lax.rsqrt(var + eps) * g + b


def _sigmoid(x):
    return 1.0 / (1.0 + jnp.exp(-x))


def _gelu_tanh(x):
    c = math.sqrt(2.0 / math.pi)
    return 0.5 * x * (1.0 + jnp.tanh(c * (x + 0.044715 * (x * x * x))))


def _const_spec(shape):
    nd = len(shape)
    return pl.BlockSpec(shape, lambda *_: (0,) * nd, pipeline_mode=pl.Buffered(1))


def _mixer0_kernel(x_ref, win_ref, bin_ref, cw_ref, cb_ref, cng_ref, cnb_ref, sng_ref, snb_ref,
                   ws_ref, sb_ref, wout_ref, bout_ref, lng_ref, lnb_ref, o_ref,
                   ybuf_ref, conv_ref, mixed_ref):
    rows = MIX0_ROWS
    step = pl.program_id(1)
    x = x_ref[0]
    xb = x.astype(BF16)

    h_a = _dot(xb, win_ref[:, 0:2 * CONV_DIM]) + bin_ref[:, 0:2 * CONV_DIM]
    y = h_a[:, 0:CONV_DIM] * _sigmoid(h_a[:, CONV_DIM:2 * CONV_DIM])

    @pl.when(step == 0)
    def _():
        ybuf_ref[0:CONV_HALO, :] = jnp.zeros((CONV_HALO, CONV_DIM), F32)

    ybuf_ref[CONV_HALO:CONV_HALO + rows, :] = y
    first = CONV_HALO - (CONV_WIDTH - 1)
    for blk in range(rows // CONV_ROW_BLOCK):
        base = blk * CONV_ROW_BLOCK
        acc = jnp.broadcast_to(cb_ref[...], (CONV_ROW_BLOCK, CONV_DIM))
        for j in range(CONV_WIDTH):
            acc = acc + cw_ref[j:j + 1, :] * ybuf_ref[base + first + j:base + first + j + CONV_ROW_BLOCK, :]
        conv_ref[base:base + CONV_ROW_BLOCK, :] = acc
    ybuf_ref[0:CONV_HALO, :] = ybuf_ref[rows:rows + CONV_HALO, :]
    ya = _layer_norm(conv_ref[...], cng_ref[...], cnb_ref[...])
    ya = ya * _sigmoid(ya)
    mixed_ref[:, 0:CONV_DIM] = ya.astype(BF16)

    h_b = _dot(xb, win_ref[:, 2 * CONV_DIM:]) + bin_ref[:, 2 * CONV_DIM:]
    u = _gelu_tanh(h_b[:, 0:SGU_DIM])
    v = _layer_norm(_gelu_tanh(h_b[:, SGU_DIM:]), sng_ref[...], snb_ref[...]).astype(BF16)
    t_idx = lax.broadcasted_iota(jnp.int32, (SGU_CHUNK, SGU_CHUNK), 0)
    s_idx = lax.broadcasted_iota(jnp.int32, (SGU_CHUNK, SGU_CHUNK), 1)
    causal = t_idx >= s_idx
    for g in range(SGU_GROUPS):
        w_s = jnp.where(causal, ws_ref[g], 0.0).astype(BF16)
        cols = slice(g * SGU_GROUP_DIM, (g + 1) * SGU_GROUP_DIM)
        for c in range(rows // SGU_CHUNK):
            rws = slice(c * SGU_CHUNK, (c + 1) * SGU_CHUNK)
            sv = _dot(w_s, v[rws, cols]) + sb_ref[g]
            mixed_ref[rws, CONV_DIM + g * SGU_GROUP_DIM:CONV_DIM + (g + 1) * SGU_GROUP_DIM] = (
                u[rws, cols] * sv).astype(BF16)

    mix = _dot(mixed_ref[...], wout_ref[...]) + bout_ref[...]
    o_ref[0] = _layer_norm(ALPHA * x + mix, lng_ref[...], lnb_ref[...])


def _mixer0(x, w_in, b_in, conv_w, conv_b, cn_g, cn_b, sn_g, sn_b, sgu_w, sgu_b_full,
            w_out, b_out, ln_g, ln_b):
    bsz, seq, d = x.shape
    rows = MIX0_ROWS
    x_spec = pl.BlockSpec((1, rows, d), lambda b, i: (b, i, 0))
    consts = (w_in, b_in, conv_w, conv_b, cn_g, cn_b, sn_g, sn_b, sgu_w, sgu_b_full,
              w_out, b_out, ln_g, ln_b)
    return pl.pallas_call(
        _mixer0_kernel,
        grid=(bsz, seq // rows),
        in_specs=[x_spec] + [_const_spec(c.shape) for c in consts],
        out_specs=x_spec,
        out_shape=jax.ShapeDtypeStruct(x.shape, F32),
        scratch_shapes=[
            pltpu.VMEM((CONV_HALO + rows, CONV_DIM), F32),
            pltpu.VMEM((rows, CONV_DIM), F32),
            pltpu.VMEM((rows, CONV_DIM + SGU_DIM), BF16),
        ],
        compiler_params=pltpu.CompilerParams(
            dimension_semantics=("arbitrary", "arbitrary"),
            vmem_limit_bytes=VMEM_LIMIT_BYTES),
        name="mixer0",
    )(x, *consts)


def _ffn_kernel(x_ref, win_ref, wout_ref, g_ref, b_ref, o_ref, act_ref):
    x = x_ref[...]
    xb = x.astype(BF16)
    for c in range(FFN_HIDDEN // FFN_COLS):
        lo = c * FFN_COLS
        gate = _dot(xb, win_ref[:, lo:lo + FFN_COLS])
        up = _dot(xb, win_ref[:, FFN_HIDDEN + lo:FFN_HIDDEN + lo + FFN_COLS])
        act_ref[:, lo:lo + FFN_COLS] = (gate * _sigmoid(gate) * up).astype(BF16)
    y = _dot(act_ref[...], wout_ref[...])
    o_ref[...] = _layer_norm(ALPHA * x + y, g_ref[...], b_ref[...])


def _ffn(x2d, w_in, w_out, ln_g, ln_b, name):
    n, d = x2d.shape
    rows = FFN_ROWS
    x_spec = pl.BlockSpec((rows, d), lambda i: (i, 0))
    consts = (w_in, w_out, ln_g, ln_b)
    return pl.pallas_call(
        _ffn_kernel,
        grid=(n // rows,),
        in_specs=[x_spec] + [_const_spec(c.shape) for c in consts],
        out_specs=x_spec,
        out_shape=jax.ShapeDtypeStruct(x2d.shape, F32),
        scratch_shapes=[pltpu.VMEM((rows, FFN_HIDDEN), BF16)],
        compiler_params=pltpu.CompilerParams(
            dimension_semantics=("arbitrary",),
            vmem_limit_bytes=VMEM_LIMIT_BYTES),
        name=name,
    )(x2d, *consts)


def _block_diag(z):
    c = z.shape[0]
    tiled = jnp.concatenate([z] * HEADS_PER_QUAD, axis=0)
    r_head = lax.broadcasted_iota(jnp.int32, tiled.shape, 0) // c
    l_head = lax.broadcasted_iota(jnp.int32, tiled.shape, 1) // HEAD
    return jnp.where(r_head == l_head, tiled, jnp.zeros_like(tiled))


def _head_sum(z, ones_bd):
    parts = [_dot(z[:, q * QUAD:(q + 1) * QUAD].astype(BF16), ones_bd)
             for q in range(z.shape[1] // QUAD)]
    return jnp.concatenate(parts, axis=1)


def _rwkv_kernel(x_ref, mu_ref, wr_ref, wk_ref, wv_ref, w0_ref, ww1_ref, ww2_ref, a0_ref,
                 aw1_ref, aw2_ref, gw1_ref, gw2_ref, kk_ref, ka_ref, rk_ref, gng_ref, gnb_ref,
                 wout_ref, lng_ref, lnb_ref, o_ref,
                 xlast_ref, state_ref, r_s, k_s, v_s, kk_s, kka_s, lw_s, g_s, y_s):
    rows = RWKV_ROWS
    chunk = RWKV_CHUNK
    d = D_MODEL
    n_quads = d // QUAD
    step = pl.program_id(1)

    @pl.when(step == 0)
    def _():
        xlast_ref[...] = jnp.zeros(xlast_ref.shape, F32)
        state_ref[...] = jnp.zeros(state_ref.shape, F32)

    x = x_ref[0]
    row_id = lax.broadcasted_iota(jnp.int32, (rows, d), 0)
    x_prev = jnp.where(row_id == 0, xlast_ref[0:1, :], pltpu.roll(x, 1, axis=0))
    xlast_ref[0:1, :] = x[rows - 1:rows, :]
    xx = x_prev - x

    def mixed(j):
        return (x + xx * mu_ref[j:j + 1, :]).astype(BF16)

    r_i = lax.broadcasted_iota(jnp.int32, (QUAD, QUAD), 0) // HEAD
    c_i = lax.broadcasted_iota(jnp.int32, (QUAD, QUAD), 1) // HEAD
    same_head = r_i == c_i
    ones_bd = jnp.where(same_head, 1.0, 0.0).astype(BF16)

    r = _dot(mixed(0), wr_ref[...])
    k = _dot(mixed(2), wk_ref[...])
    v = _dot(mixed(3), wv_ref[...])
    z_w = w0_ref[...] + _dot(jnp.tanh(_dot(mixed(1), ww1_ref[...])).astype(BF16), ww2_ref[...])
    lw_s[...] = -math.exp(-0.5) * _sigmoid(z_w)
    a = _sigmoid(a0_ref[...] + _dot(_dot(mixed(4), aw1_ref[...]).astype(BF16), aw2_ref[...]))
    kk = k * kk_ref[...]
    kk = kk / jnp.maximum(jnp.sqrt(_head_sum(kk * kk, ones_bd)), 1e-12)
    k2 = k * (1.0 + (a - 1.0) * ka_ref[...])
    g_s[...] = _dot(_sigmoid(_dot(mixed(5), gw1_ref[...])).astype(BF16), gw2_ref[...])
    r_s[...] = r
    k_s[...] = k2
    v_s[...] = v
    kk_s[...] = kk
    kka_s[...] = kk * a

    t_i = lax.broadcasted_iota(jnp.int32, (chunk, chunk), 0)
    s_i = lax.broadcasted_iota(jnp.int32, (chunk, chunk), 1)
    tri_incl = jnp.where(t_i >= s_i, 1.0, 0.0).astype(BF16)
    t_n = lax.broadcasted_iota(jnp.int32, (chunk, QUAD), 0)
    s_n = lax.broadcasted_iota(jnp.int32, (chunk, QUAD), 1) % HEAD
    strict = t_n > s_n
    incl = t_n >= s_n
    eye_nat = jnp.where(t_n == s_n, 1.0, 0.0).astype(F32)

    for c in range(rows // chunk):
        rws = slice(c * chunk, (c + 1) * chunk)
        lw = lw_s[rws, :]
        lw_hi = lw.astype(BF16)
        lw_lo = (lw - lw_hi.astype(F32)).astype(BF16)
        cum = _dot(tri_incl, lw_hi) + _dot(tri_incl, lw_lo)
        tot = cum[chunk - 1:chunk, :]
        gam = jnp.exp(cum)
        gam_inv = jnp.exp(-cum)
        gam_prev = jnp.exp(cum - lw)
        gam_rest = jnp.exp(tot - cum)
        gam_tot = jnp.exp(tot)
        kk_c = kk_s[rws, :]
        kka_c = kka_s[rws, :]
        k_c = k_s[rws, :]
        v_c = v_s[rws, :]
        a_t = (-kk_c * gam_prev).astype(BF16)
        r_t = (r_s[rws, :] * gam).astype(BF16)
        b_t = (kka_c * gam_inv).astype(BF16)
        k_t = (k_c * gam_inv).astype(BF16)
        b_h = (kka_c * gam_rest).astype(BF16)
        k_h = (k_c * gam_rest).astype(BF16)
        v_b = v_c.astype(BF16)
        for q in range(n_quads):
            ql = slice(q * QUAD, (q + 1) * QUAD)
            ar = jnp.concatenate([a_t[:, ql], r_t[:, ql]], axis=0)
            s_b = _dot_nt(ar, _block_diag(b_t[:, ql]))
            s_k = _dot_nt(ar, _block_diag(k_t[:, ql]))
            ab = jnp.where(strict, s_b[0:chunk], 0.0)
            rb = jnp.where(incl, s_b[chunk:], 0.0).astype(BF16)
            ak = jnp.where(strict, s_k[0:chunk], 0.0).astype(BF16)
            rk = jnp.where(incl, s_k[chunk:], 0.0).astype(BF16)
            p_inv = eye_nat + ab
            l_pow = ab
            for _ in range(int(math.log2(chunk)) - 1):
                l_pow_b = l_pow.astype(BF16)
                l_pow = _dot(l_pow_b, _block_diag(l_pow_b))
                p_inv = p_inv + _dot(l_pow.astype(BF16), _block_diag(p_inv.astype(BF16)))
            s0 = state_ref[q]
            s0_b = s0.astype(BF16)
            v_bd = _block_diag(v_b[:, ql])
            x_in = _dot_nt(a_t[:, ql], s0_b) + _dot(ak, v_bd)
            u = _dot(p_inv.astype(BF16), _block_diag(x_in.astype(BF16)))
            y = (_dot_nt(r_t[:, ql], s0_b) + _dot(rb, _block_diag(u.astype(BF16)))
                 + _dot(rk, v_bd))
            y_s[rws, ql] = y
            uv_t = jnp.concatenate([u, v_c[:, ql]], axis=0).T.astype(BF16)
            bk = jnp.concatenate([b_h[:, ql], k_h[:, ql]], axis=0)
            upd = _dot(uv_t, bk)
            state_ref[q] = s0 * gam_tot[:, ql] + jnp.where(same_head, upd, 0.0)

    y = y_s[...]
    y_c = y - _head_sum(y, ones_bd) * (1.0 / HEAD)
    y_var = _head_sum(y_c * y_c, ones_bd) * (1.0 / HEAD)
    y_n = y_c * lax.rsqrt(y_var + GN_EPS) * gng_ref[...] + gnb_ref[...]
    bonus = _head_sum(r_s[...] * k_s[...] * rk_ref[...], ones_bd) * v_s[...]
    mix = _dot(((y_n + bonus) * g_s[...]).astype(BF16), wout_ref[...])
    o_ref[0] = _layer_norm(ALPHA * x_ref[0] + mix, lng_ref[...], lnb_ref[...])


def _rwkv(x, consts):
    bsz, seq, d = x.shape
    rows = RWKV_ROWS
    x_spec = pl.BlockSpec((1, rows, d), lambda b, i: (b, i, 0))
    tile = pltpu.VMEM((rows, d), F32)
    return pl.pallas_call(
        _rwkv_kernel,
        grid=(bsz, seq // rows),
        in_specs=[x_spec] + [_const_spec(c.shape) for c in consts],
        out_specs=x_spec,
        out_shape=jax.ShapeDtypeStruct(x.shape, F32),
        scratch_shapes=[
            pltpu.VMEM((8, d), F32),
            pltpu.VMEM((d // QUAD, QUAD, QUAD), F32),
            tile, tile, tile, tile, tile, tile, tile, tile,
        ],
        compiler_params=pltpu.CompilerParams(
            dimension_semantics=("arbitrary", "arbitrary"),
            vmem_limit_bytes=VMEM_LIMIT_BYTES),
        name="rwkv7",
    )(x, *consts)


def _row(p):
    return p.reshape(1, -1).astype(F32)


def kernel(x, ab_w_in, ab_b_in, conv_w, conv_b, conv_norm_g, conv_norm_b, sgu_norm_g, sgu_norm_b, sgu_w, sgu_b, ab_w_out, ab_b_out, rwkv_mu, rwkv_w_rkv, rwkv_w0, rwkv_w_w1, rwkv_w_w2, rwkv_a0, rwkv_a_w1, rwkv_a_w2, rwkv_g_w1, rwkv_g_w2, rwkv_k_k, rwkv_k_a, rwkv_r_k, rwkv_ln_g, rwkv_ln_b, rwkv_w_out, ffn_w_in, ffn_w_out, ln_mix_g, ln_mix_b, ln_ffn_g, ln_ffn_b):
    bsz, seq, d = x.shape
    for layer in range(DEPTH):
        i = layer // 2
        if layer % 2 == 0:
            sgu_b_full = jnp.broadcast_to(sgu_b[i][:, :, None], (SGU_GROUPS, SGU_CHUNK, SGU_GROUP_DIM))
            x = _mixer0(
                x, ab_w_in[i].astype(BF16), _row(ab_b_in[i]),
                conv_w[i].reshape(CONV_WIDTH, CONV_DIM), _row(conv_b[i]),
                _row(conv_norm_g[i]), _row(conv_norm_b[i]), _row(sgu_norm_g[i]), _row(sgu_norm_b[i]),
                sgu_w[i], sgu_b_full, ab_w_out[i].astype(BF16), _row(ab_b_out[i]),
                _row(ln_mix_g[layer]), _row(ln_mix_b[layer]))
        else:
            consts = (
                rwkv_mu[i], rwkv_w_rkv[i, 0].astype(BF16), rwkv_w_rkv[i, 1].astype(BF16),
                rwkv_w_rkv[i, 2].astype(BF16), _row(rwkv_w0[i]), rwkv_w_w1[i].astype(BF16),
                rwkv_w_w2[i].astype(BF16), _row(rwkv_a0[i]), rwkv_a_w1[i].astype(BF16),
                rwkv_a_w2[i].astype(BF16), rwkv_g_w1[i].astype(BF16), rwkv_g_w2[i].astype(BF16),
                _row(rwkv_k_k[i]), _row(rwkv_k_a[i]), _row(rwkv_r_k[i]), _row(rwkv_ln_g[i]),
                _row(rwkv_ln_b[i]), rwkv_w_out[i].astype(BF16),
                _row(ln_mix_g[layer]), _row(ln_mix_b[layer]))
            x = _rwkv(x, consts)
        x = _ffn(x.reshape(bsz * seq, d), ffn_w_in[layer].astype(BF16), ffn_w_out[layer].astype(BF16),
                 _row(ln_ffn_g[layer]), _row(ln_ffn_b[layer]), name=f"ffn{layer}").reshape(bsz, seq, d)
    return x
```

```python
import functools
import math

import jax
import jax.numpy as jnp
from jax import lax
from jax.experimental import pallas as pl
from jax.experimental.pallas import tpu as pltpu

F32 = jnp.float32
BF16 = jnp.bfloat16

D_MODEL = 1024
DEPTH = 2
ALPHA = (2.0 * DEPTH) ** 0.25
LN_EPS = 1e-5

CONV_DIM = 512
CONV_WIDTH = 31
SGU_DIM = 512
SGU_GROUPS = 4
SGU_GROUP_DIM = SGU_DIM // SGU_GROUPS
SGU_CHUNK = 128

HEAD = 64
GN_EPS = 64e-5
FFN_HIDDEN = 2816

LANES = 128
MXU_DIM = 256
VMEM_LIMIT_BYTES = 56 * 1024 * 1024

MIX0_ROWS = 512
CONV_HALO = 32
CONV_ROW_BLOCK = 32
FFN_ROWS = 512
FFN_COLS = 256
RWKV_ROWS = 256
RWKV_CHUNK = 64
QUAD = MXU_DIM
HEADS_PER_QUAD = QUAD // HEAD


def _dot(a, b):
    return jnp.dot(a, b, preferred_element_type=F32)


def _dot_nt(a, b):
    return lax.dot_general(a, b, (((1,), (1,)), ((), ())), preferred_element_type=F32)


def _layer_norm(x, g, b, eps=LN_EPS):
    mu = jnp.mean(x, axis=-1, keepdims=True)
    xc = x - mu
    var = jnp.mean(xc * xc, axis=-1, keepdims=True)
    return xc * lax.rsqrt(var + eps) * g + b


def _sigmoid(x):
    return 1.0 / (1.0 + jnp.exp(-x))


def _gelu_tanh(x):
    c = math.sqrt(2.0 / math.pi)
    return 0.5 * x * (1.0 + jnp.tanh(c * (x + 0.044715 * (x * x * x))))


def _const_spec(shape):
    nd = len(shape)
    return pl.BlockSpec(shape, lambda *_: (0,) * nd, pipeline_mode=pl.Buffered(1))


def _mixer0_kernel(x_ref, win_ref, bin_ref, cw_ref, cb_ref, cng_ref, cnb_ref, sng_ref, snb_ref,
                   ws_ref, sb_ref, wout_ref, bout_ref, lng_ref, lnb_ref, o_ref,
                   ybuf_ref, conv_ref, mixed_ref):
    rows = MIX0_ROWS
    step = pl.program_id(1)
    x = x_ref[0]
    xb = x.astype(BF16)

    h_a = _dot(xb, win_ref[:, 0:2 * CONV_DIM]) + bin_ref[:, 0:2 * CONV_DIM]
    y = h_a[:, 0:CONV_DIM] * _sigmoid(h_a[:, CONV_DIM:2 * CONV_DIM])

    @pl.when(step == 0)
    def _():
        ybuf_ref[0:CONV_HALO, :] = jnp.zeros((CONV_HALO, CONV_DIM), F32)

    ybuf_ref[CONV_HALO:CONV_HALO + rows, :] = y
    first = CONV_HALO - (CONV_WIDTH - 1)
    for blk in range(rows // CONV_ROW_BLOCK):
        base = blk * CONV_ROW_BLOCK
        acc = jnp.broadcast_to(cb_ref[...], (CONV_ROW_BLOCK, CONV_DIM))
        for j in range(CONV_WIDTH):
            acc = acc + cw_ref[j:j + 1, :] * ybuf_ref[base + first + j:base + first + j + CONV_ROW_BLOCK, :]
        conv_ref[base:base + CONV_ROW_BLOCK, :] = acc
    ybuf_ref[0:CONV_HALO, :] = ybuf_ref[rows:rows + CONV_HALO, :]
    ya = _layer_norm(conv_ref[...], cng_ref[...], cnb_ref[...])
    ya = ya * _sigmoid(ya)
    mixed_ref[:, 0:CONV_DIM] = ya.astype(BF16)

    h_b = _dot(xb, win_ref[:, 2 * CONV_DIM:]) + bin_ref[:, 2 * CONV_DIM:]
    u = _gelu_tanh(h_b[:, 0:SGU_DIM])
    v = _layer_norm(_gelu_tanh(h_b[:, SGU_DIM:]), sng_ref[...], snb_ref[...]).astype(BF16)
    t_idx = lax.broadcasted_iota(jnp.int32, (SGU_CHUNK, SGU_CHUNK), 0)
    s_idx = lax.broadcasted_iota(jnp.int32, (SGU_CHUNK, SGU_CHUNK), 1)
    causal = t_idx >= s_idx
    for g in range(SGU_GROUPS):
        w_s = jnp.where(causal, ws_ref[g], 0.0).astype(BF16)
        cols = slice(g * SGU_GROUP_DIM, (g + 1) * SGU_GROUP_DIM)
        for c in range(rows // SGU_CHUNK):
            rws = slice(c * SGU_CHUNK, (c + 1) * SGU_CHUNK)
            sv = _dot(w_s, v[rws, cols]) + sb_ref[g]
            mixed_ref[rws, CONV_DIM + g * SGU_GROUP_DIM:CONV_DIM + (g + 1) * SGU_GROUP_DIM] = (
                u[rws, cols] * sv).astype(BF16)

    mix = _dot(mixed_ref[...], wout_ref[...]) + bout_ref[...]
    o_ref[0] = _layer_norm(ALPHA * x + mix, lng_ref[...], lnb_ref[...])


def _mixer0(x, w_in, b_in, conv_w, conv_b, cn_g, cn_b, sn_g, sn_b, sgu_w, sgu_b_full,
            w_out, b_out, ln_g, ln_b):
    bsz, seq, d = x.shape
    rows = MIX0_ROWS
    x_spec = pl.BlockSpec((1, rows, d), lambda b, i: (b, i, 0))
    consts = (w_in, b_in, conv_w, conv_b, cn_g, cn_b, sn_g, sn_b, sgu_w, sgu_b_full,
              w_out, b_out, ln_g, ln_b)
    return pl.pallas_call(
        _mixer0_kernel,
        grid=(bsz, seq // rows),
        in_specs=[x_spec] + [_const_spec(c.shape) for c in consts],
        out_specs=x_spec,
        out_shape=jax.ShapeDtypeStruct(x.shape, F32),
        scratch_shapes=[
            pltpu.VMEM((CONV_HALO + rows, CONV_DIM), F32),
            pltpu.VMEM((rows, CONV_DIM), F32),
            pltpu.VMEM((rows, CONV_DIM + SGU_DIM), BF16),
        ],
        compiler_params=pltpu.CompilerParams(
            dimension_semantics=("arbitrary", "arbitrary"),
            vmem_limit_bytes=VMEM_LIMIT_BYTES),
        name="mixer0",
    )(x, *consts)


def _ffn_kernel(x_ref, win_ref, wout_ref, g_ref, b_ref, o_ref, act_ref):
    x = x_ref[...]
    xb = x.astype(BF16)
    for c in range(FFN_HIDDEN // FFN_COLS):
        lo = c * FFN_COLS
        gate = _dot(xb, win_ref[:, lo:lo + FFN_COLS])
        up = _dot(xb, win_ref[:, FFN_HIDDEN + lo:FFN_HIDDEN + lo + FFN_COLS])
        act_ref[:, lo:lo + FFN_COLS] = (gate * _sigmoid(gate) * up).astype(BF16)
    y = _dot(act_ref[...], wout_ref[...])
    o_ref[...] = _layer_norm(ALPHA * x + y, g_ref[...], b_ref[...])


def _ffn(x2d, w_in, w_out, ln_g, ln_b, name):
    n, d = x2d.shape
    rows = FFN_ROWS
    x_spec = pl.BlockSpec((rows, d), lambda i: (i, 0))
    consts = (w_in, w_out, ln_g, ln_b)
    return pl.pallas_call(
        _ffn_kernel,
        grid=(n // rows,),
        in_specs=[x_spec] + [_const_spec(c.shape) for c in consts],
        out_specs=x_spec,
        out_shape=jax.ShapeDtypeStruct(x2d.shape, F32),
        scratch_shapes=[pltpu.VMEM((rows, FFN_HIDDEN), BF16)],
        compiler_params=pltpu.CompilerParams(
            dimension_semantics=("arbitrary",),
            vmem_limit_bytes=VMEM_LIMIT_BYTES),
        name=name,
    )(x2d, *consts)


def _block_diag(z):
    c = z.shape[0]
    tiled = jnp.concatenate([z] * HEADS_PER_QUAD, axis=0)
    r_head = lax.broadcasted_iota(jnp.int32, tiled.shape, 0) // c
    l_head = lax.broadcasted_iota(jnp.int32, tiled.shape, 1) // HEAD
    return jnp.where(r_head == l_head, tiled, jnp.zeros_like(tiled))


def _head_sum(z, ones_bd):
    parts = [_dot(z[:, q * QUAD:(q + 1) * QUAD].astype(BF16), ones_bd)
             for q in range(z.shape[1] // QUAD)]
    return jnp.concatenate(parts, axis=1)


def _rwkv_kernel(x_ref, mu_ref, wr_ref, wk_ref, wv_ref, w0_ref, ww1_ref, ww2_ref, a0_ref,
                 aw1_ref, aw2_ref, gw1_ref, gw2_ref, kk_ref, ka_ref, rk_ref, gng_ref, gnb_ref,
                 wout_ref, lng_ref, lnb_ref, o_ref,
                 xlast_ref, state_ref, r_s, k_s, v_s, kk_s, kka_s, lw_s, g_s, y_s):
    rows = RWKV_ROWS
    chunk = RWKV_CHUNK
    d = D_MODEL
    n_quads = d // QUAD
    step = pl.program_id(1)

    @pl.when(step == 0)
    def _():
        xlast_ref[...] = jnp.zeros(xlast_ref.shape, F32)
        state_ref[...] = jnp.zeros(state_ref.shape, F32)

    x = x_ref[0]
    row_id = lax.broadcasted_iota(jnp.int32, (rows, d), 0)
    x_prev = jnp.where(row_id == 0, xlast_ref[0:1, :], pltpu.roll(x, 1, axis=0))
    xlast_ref[0:1, :] = x[rows - 1:rows, :]
    xx = x_prev - x

    def mixed(j):
        return (x + xx * mu_ref[j:j + 1, :]).astype(BF16)

    r_i = lax.broadcasted_iota(jnp.int32, (QUAD, QUAD), 0) // HEAD
    c_i = lax.broadcasted_iota(jnp.int32, (QUAD, QUAD), 1) // HEAD
    same_head = r_i == c_i
    ones_bd = jnp.where(same_head, 1.0, 0.0).astype(BF16)

    r = _dot(mixed(0), wr_ref[...])
    k = _dot(mixed(2), wk_ref[...])
    v = _dot(mixed(3), wv_ref[...])
    z_w = w0_ref[...] + _dot(jnp.tanh(_dot(mixed(1), ww1_ref[...])).astype(BF16), ww2_ref[...])
    lw_s[...] = -math.exp(-0.5) * _sigmoid(z_w)
    a = _sigmoid(a0_ref[...] + _dot(_dot(mixed(4), aw1_ref[...]).astype(BF16), aw2_ref[...]))
    kk = k * kk_ref[...]
    kk = kk / jnp.maximum(jnp.sqrt(_head_sum(kk * kk, ones_bd)), 1e-12)
    k2 = k * (1.0 + (a - 1.0) * ka_ref[...])
    g_s[...] = _dot(_sigmoid(_dot(mixed(5), gw1_ref[...])).astype(BF16), gw2_ref[...])
    r_s[...] = r
    k_s[...] = k2
    v_s[...] = v
    kk_s[...] = kk
    kka_s[...] = kk * a

    t_i = lax.broadcasted_iota(jnp.int32, (chunk, chunk), 0)
    s_i = lax.broadcasted_iota(jnp.int32, (chunk, chunk), 1)
    tri_incl = jnp.where(t_i >= s_i, 1.0, 0.0).astype(BF16)
    t_n = lax.broadcasted_iota(jnp.int32, (chunk, QUAD), 0)
    s_n = lax.broadcasted_iota(jnp.int32, (chunk, QUAD), 1) % HEAD
    strict = t_n > s_n
    incl = t_n >= s_n
    eye_nat = jnp.where(t_n == s_n, 1.0, 0.0).astype(F32)

    n_chunks = rows // chunk
    items = [(c, q) for c in range(n_chunks) for q in range(n_quads)]
    quad = lambda z, q: z[:, q * QUAD:(q + 1) * QUAD]

    prep = []
    for c in range(n_chunks):
        rws = slice(c * chunk, (c + 1) * chunk)
        lw = lw_s[rws, :]
        lw_hi = lw.astype(BF16)
        lw_lo = (lw - lw_hi.astype(F32)).astype(BF16)
        cum = _dot(tri_incl, lw_hi) + _dot(tri_incl, lw_lo)
        tot = cum[chunk - 1:chunk, :]
        gam_inv = jnp.exp(-cum)
        gam_rest = jnp.exp(tot - cum)
        kk_c = kk_s[rws, :]
        kka_c = kka_s[rws, :]
        k_c = k_s[rws, :]
        prep.append(dict(
            a_t=(-kk_c * jnp.exp(cum - lw)).astype(BF16),
            r_t=(r_s[rws, :] * jnp.exp(cum)).astype(BF16),
            b_t=(kka_c * gam_inv).astype(BF16),
            k_t=(k_c * gam_inv).astype(BF16),
            b_h=(kka_c * gam_rest).astype(BF16),
            k_h=(k_c * gam_rest).astype(BF16),
            gam_tot=jnp.exp(tot)))
    s_b = [_dot_nt(jnp.concatenate([quad(prep[c]["a_t"], q), quad(prep[c]["r_t"], q)], axis=0),
                   _block_diag(quad(prep[c]["b_t"], q))) for c, q in items]
    s_k = [_dot_nt(jnp.concatenate([quad(prep[c]["a_t"], q), quad(prep[c]["r_t"], q)], axis=0),
                   _block_diag(quad(prep[c]["k_t"], q))) for c, q in items]
    rb = [jnp.where(incl, z[chunk:], 0.0).astype(BF16) for z in s_b]
    ak = [jnp.where(strict, z[0:chunk], 0.0).astype(BF16) for z in s_k]
    rk = [jnp.where(incl, z[chunk:], 0.0).astype(BF16) for z in s_k]
    l_pow = [jnp.where(strict, z[0:chunk], 0.0) for z in s_b]
    p_inv = [eye_nat + z for z in l_pow]
    for _ in range(int(math.log2(chunk)) - 1):
        l_pow_b = [z.astype(BF16) for z in l_pow]
        l_pow = [_dot(z, _block_diag(z)) for z in l_pow_b]
        p_inv = [p + _dot(z.astype(BF16), _block_diag(p.astype(BF16))) for p, z in zip(p_inv, l_pow)]
    p_inv = [p.astype(BF16) for p in p_inv]

    for c in range(n_chunks):
        rws = slice(c * chunk, (c + 1) * chunk)
        pc = prep[c]
        v_c = v_s[rws, :]
        v_b = v_c.astype(BF16)
        idx = [c * n_quads + q for q in range(n_quads)]
        s0 = [state_ref[q] for q in range(n_quads)]
        s0_b = [z.astype(BF16) for z in s0]
        v_bd = [_block_diag(quad(v_b, q)) for q in range(n_quads)]
        x_in = [_dot_nt(quad(pc["a_t"], q), s0_b[q]) + _dot(ak[idx[q]], v_bd[q]) for q in range(n_quads)]
        u = [_dot(p_inv[idx[q]], _block_diag(x_in[q].astype(BF16))) for q in range(n_quads)]
        for q in range(n_quads):
            y_s[rws, q * QUAD:(q + 1) * QUAD] = (
                _dot_nt(quad(pc["r_t"], q), s0_b[q]) + _dot(rb[idx[q]], _block_diag(u[q].astype(BF16)))
                + _dot(rk[idx[q]], v_bd[q]))
        for q in range(n_quads):
            uv_t = jnp.concatenate([u[q], quad(v_c, q)], axis=0).T.astype(BF16)
            bk = jnp.concatenate([quad(pc["b_h"], q), quad(pc["k_h"], q)], axis=0)
            state_ref[q] = s0[q] * quad(pc["gam_tot"], q) + jnp.where(same_head, _dot(uv_t, bk), 0.0)

    y = y_s[...]
    y_c = y - _head_sum(y, ones_bd) * (1.0 / HEAD)
    y_var = _head_sum(y_c * y_c, ones_bd) * (1.0 / HEAD)
    y_n = y_c * lax.rsqrt(y_var + GN_EPS) * gng_ref[...] + gnb_ref[...]
    bonus = _head_sum(r_s[...] * k_s[...] * rk_ref[...], ones_bd) * v_s[...]
    mix = _dot(((y_n + bonus) * g_s[...]).astype(BF16), wout_ref[...])
    o_ref[0] = _layer_norm(ALPHA * x_ref[0] + mix, lng_ref[...], lnb_ref[...])


def _rwkv(x, consts):
    bsz, seq, d = x.shape
    rows = RWKV_ROWS
    x_spec = pl.BlockSpec((1, rows, d), lambda b, i: (b, i, 0))
    tile = pltpu.VMEM((rows, d), F32)
    return pl.pallas_call(
        _rwkv_kernel,
        grid=(bsz, seq // rows),
        in_specs=[x_spec] + [_const_spec(c.shape) for c in consts],
        out_specs=x_spec,
        out_shape=jax.ShapeDtypeStruct(x.shape, F32),
        scratch_shapes=[
            pltpu.VMEM((8, d), F32),
            pltpu.VMEM((d // QUAD, QUAD, QUAD), F32),
            tile, tile, tile, tile, tile, tile, tile, tile,
        ],
        compiler_params=pltpu.CompilerParams(
            dimension_semantics=("arbitrary", "arbitrary"),
            vmem_limit_bytes=VMEM_LIMIT_BYTES),
        name="rwkv7",
    )(x, *consts)


def _row(p):
    return p.reshape(1, -1).astype(F32)


def kernel(x, ab_w_in, ab_b_in, conv_w, conv_b, conv_norm_g, conv_norm_b, sgu_norm_g, sgu_norm_b, sgu_w, sgu_b, ab_w_out, ab_b_out, rwkv_mu, rwkv_w_rkv, rwkv_w0, rwkv_w_w1, rwkv_w_w2, rwkv_a0, rwkv_a_w1, rwkv_a_w2, rwkv_g_w1, rwkv_g_w2, rwkv_k_k, rwkv_k_a, rwkv_r_k, rwkv_ln_g, rwkv_ln_b, rwkv_w_out, ffn_w_in, ffn_w_out, ln_mix_g, ln_mix_b, ln_ffn_g, ln_ffn_b):
    bsz, seq, d = x.shape
    for layer in range(DEPTH):
        i = layer // 2
        if layer % 2 == 0:
            sgu_b_full = jnp.broadcast_to(sgu_b[i][:, :, None], (SGU_GROUPS, SGU_CHUNK, SGU_GROUP_DIM))
            x = _mixer0(
                x, ab_w_in[i].astype(BF16), _row(ab_b_in[i]),
                conv_w[i].reshape(CONV_WIDTH, CONV_DIM), _row(conv_b[i]),
                _row(conv_norm_g[i]), _row(conv_norm_b[i]), _row(sgu_norm_g[i]), _row(sgu_norm_b[i]),
                sgu_w[i], sgu_b_full, ab_w_out[i].astype(BF16), _row(ab_b_out[i]),
                _row(ln_mix_g[layer]), _row(ln_mix_b[layer]))
        else:
            consts = (
                rwkv_mu[i], rwkv_w_rkv[i, 0].astype(BF16), rwkv_w_rkv[i, 1].astype(BF16),
                rwkv_w_rkv[i, 2].astype(BF16), _row(rwkv_w0[i]), rwkv_w_w1[i].astype(BF16),
                rwkv_w_w2[i].astype(BF16), _row(rwkv_a0[i]), rwkv_a_w1[i].astype(BF16),
                rwkv_a_w2[i].astype(BF16), rwkv_g_w1[i].astype(BF16), rwkv_g_w2[i].astype(BF16),
                _row(rwkv_k_k[i]), _row(rwkv_k_a[i]), _row(rwkv_r_k[i]), _row(rwkv_ln_g[i]),
                _row(rwkv_ln_b[i]), rwkv_w_out[i].astype(BF16),
                _row(ln_mix_g[layer]), _row(ln_mix_b[layer]))
            x = _rwkv(x, consts)
        x = _ffn(x.reshape(bsz * seq, d), ffn_w_in[layer].astype(BF16), ffn_w_out[layer].astype(BF16),
                 _row(ln_ffn_g[layer]), _row(ln_ffn_b[layer]), name=f"ffn{layer}").reshape(bsz, seq, d)
    return x
```

```python
import functools
import math

import jax
import jax.numpy as jnp
from jax import lax
from jax.experimental import pallas as pl
from jax.experimental.pallas import tpu as pltpu

F32 = jnp.float32
BF16 = jnp.bfloat16

D_MODEL = 1024
DEPTH = 2
ALPHA = (2.0 * DEPTH) ** 0.25
LN_EPS = 1e-5

CONV_DIM = 512
CONV_WIDTH = 31
SGU_DIM = 512
SGU_GROUPS = 4
SGU_GROUP_DIM = SGU_DIM // SGU_GROUPS
SGU_CHUNK = 128

HEAD = 64
GN_EPS = 64e-5
FFN_HIDDEN = 2816

LANES = 128
SUBLANES = 8
MXU_DIM = 256
VMEM_LIMIT_BYTES = 56 * 1024 * 1024

MIX0_ROWS = 512
CONV_HALO = 32
CONV_ROW_BLOCK = 32
FFN_ROWS = 512
FFN_COLS = 256
RWKV_ROWS = 256
RWKV_CHUNK = 64
RWKV_GROUP = 2
QUAD = MXU_DIM
HEADS_PER_QUAD = QUAD // HEAD


def _dot(a, b):
    return jnp.dot(a, b, preferred_element_type=F32)


def _dot_nt(a, b):
    return lax.dot_general(a, b, (((1,), (1,)), ((), ())), preferred_element_type=F32)


def _layer_norm(x, g, b, eps=LN_EPS):
    mu = jnp.mean(x, axis=-1, keepdims=True)
    xc = x - mu
    var = jnp.mean(xc * xc, axis=-1, keepdims=True)
    return xc * lax.rsqrt(var + eps) * g + b


def _sigmoid(x):
    return 1.0 / (1.0 + jnp.exp(-x))


def _gelu_tanh(x):
    c = math.sqrt(2.0 / math.pi)
    return 0.5 * x * (1.0 + jnp.tanh(c * (x + 0.044715 * (x * x * x))))


def _const_spec(shape):
    nd = len(shape)
    return pl.BlockSpec(shape, lambda *_: (0,) * nd, pipeline_mode=pl.Buffered(1))


def _mixer0_kernel(x_ref, win_ref, bin_ref, cw_ref, cb_ref, cng_ref, cnb_ref, sng_ref, snb_ref,
                   ws_ref, sb_ref, wout_ref, bout_ref, lng_ref, lnb_ref, o_ref,
                   ybuf_ref, yrot_ref, h_ref, mixed_ref):
    rows = MIX0_ROWS
    blk_rows = SGU_CHUNK
    n_blocks = rows // blk_rows
    step = pl.program_id(1)

    @pl.when(step == 0)
    def _():
        ybuf_ref[0:CONV_HALO, :] = jnp.zeros((CONV_HALO, CONV_DIM), F32)

    t_idx = lax.broadcasted_iota(jnp.int32, (SGU_CHUNK, SGU_CHUNK), 0)
    s_idx = lax.broadcasted_iota(jnp.int32, (SGU_CHUNK, SGU_CHUNK), 1)
    w_s = [jnp.where(t_idx >= s_idx, ws_ref[g], 0.0).astype(BF16) for g in range(SGU_GROUPS)]
    first = CONV_HALO - (CONV_WIDTH - 1)
    reach = blk_rows + CONV_HALO - SUBLANES

    def project(b):
        rws = slice(b * blk_rows, (b + 1) * blk_rows)
        h_ref[rws, :] = _dot(x_ref[0, rws, :].astype(BF16), win_ref[...]) + bin_ref[...]

    def mix_block(b):
        r0 = b * blk_rows
        rws = slice(r0, r0 + blk_rows)
        ybuf_ref[CONV_HALO + r0:CONV_HALO + r0 + blk_rows, :] = (
            h_ref[rws, 0:CONV_DIM] * _sigmoid(h_ref[rws, CONV_DIM:2 * CONV_DIM]))
        lo = 0 if b == 0 else r0 + reach - blk_rows
        for r in range(1, SUBLANES):
            yrot_ref[r - 1, lo:r0 + reach, :] = ybuf_ref[lo + r:r0 + reach + r, :]
        for sub in range(blk_rows // CONV_ROW_BLOCK):
            base = r0 + sub * CONV_ROW_BLOCK
            acc = jnp.broadcast_to(cb_ref[...], (CONV_ROW_BLOCK, CONV_DIM))
            for j in range(CONV_WIDTH):
                r = (first + j) % SUBLANES
                src = ybuf_ref if r == 0 else yrot_ref.at[r - 1]
                start = base + first + j - r
                acc = acc + cw_ref[j] * src[start:start + CONV_ROW_BLOCK, :]
            ya = _layer_norm(acc, cng_ref[...], cnb_ref[...])
            mixed_ref[base:base + CONV_ROW_BLOCK, 0:CONV_DIM] = (ya * _sigmoid(ya)).astype(BF16)
        u = _gelu_tanh(h_ref[rws, 2 * CONV_DIM:2 * CONV_DIM + SGU_DIM])
        v = _layer_norm(_gelu_tanh(h_ref[rws, 2 * CONV_DIM + SGU_DIM:]), sng_ref[...], snb_ref[...]).astype(BF16)
        for g in range(SGU_GROUPS):
            cols = slice(g * SGU_GROUP_DIM, (g + 1) * SGU_GROUP_DIM)
            sv = _dot(w_s[g], v[:, cols]) + sb_ref[g]
            mixed_ref[rws, CONV_DIM + g * SGU_GROUP_DIM:CONV_DIM + (g + 1) * SGU_GROUP_DIM] = (
                u[:, cols] * sv).astype(BF16)
        mix = _dot(mixed_ref[rws, :], wout_ref[...]) + bout_ref[...]
        o_ref[0, rws, :] = _layer_norm(ALPHA * x_ref[0, rws, :] + mix, lng_ref[...], lnb_ref[...])

    project(0)
    for b in range(n_blocks):
        if b + 1 < n_blocks:
            project(b + 1)
        mix_block(b)
    ybuf_ref[0:CONV_HALO, :] = ybuf_ref[rows:rows + CONV_HALO, :]


def _mixer0(x, w_in, b_in, conv_w, conv_b, cn_g, cn_b, sn_g, sn_b, sgu_w, sgu_b_full,
            w_out, b_out, ln_g, ln_b):
    bsz, seq, d = x.shape
    rows = MIX0_ROWS
    x_spec = pl.BlockSpec((1, rows, d), lambda b, i: (b, i, 0))
    consts = (w_in, b_in, conv_w, conv_b, cn_g, cn_b, sn_g, sn_b, sgu_w, sgu_b_full,
              w_out, b_out, ln_g, ln_b)
    return pl.pallas_call(
        _mixer0_kernel,
        grid=(bsz, seq // rows),
        in_specs=[x_spec] + [_const_spec(c.shape) for c in consts],
        out_specs=x_spec,
        out_shape=jax.ShapeDtypeStruct(x.shape, F32),
        scratch_shapes=[
            pltpu.VMEM((CONV_HALO + rows, CONV_DIM), F32),
            pltpu.VMEM((SUBLANES - 1, CONV_HALO + rows - SUBLANES, CONV_DIM), F32),
            pltpu.VMEM((rows, 2 * CONV_DIM + 2 * SGU_DIM), F32),
            pltpu.VMEM((rows, CONV_DIM + SGU_DIM), BF16),
        ],
        compiler_params=pltpu.CompilerParams(
            dimension_semantics=("arbitrary", "arbitrary"),
            vmem_limit_bytes=VMEM_LIMIT_BYTES),
        name="mixer0",
    )(x, *consts)


def _ffn_kernel(x_ref, win_ref, wout_ref, g_ref, b_ref, o_ref, act_ref):
    x = x_ref[...]
    xb = x.astype(BF16)
    for c in range(FFN_HIDDEN // FFN_COLS):
        lo = c * FFN_COLS
        gate = _dot(xb, win_ref[:, lo:lo + FFN_COLS])
        up = _dot(xb, win_ref[:, FFN_HIDDEN + lo:FFN_HIDDEN + lo + FFN_COLS])
        act_ref[:, lo:lo + FFN_COLS] = (gate * _sigmoid(gate) * up).astype(BF16)
    y = _dot(act_ref[...], wout_ref[...])
    o_ref[...] = _layer_norm(ALPHA * x + y, g_ref[...], b_ref[...])


def _ffn(x2d, w_in, w_out, ln_g, ln_b, name):
    n, d = x2d.shape
    rows = FFN_ROWS
    x_spec = pl.BlockSpec((rows, d), lambda i: (i, 0))
    consts = (w_in, w_out, ln_g, ln_b)
    return pl.pallas_call(
        _ffn_kernel,
        grid=(n // rows,),
        in_specs=[x_spec] + [_const_spec(c.shape) for c in consts],
        out_specs=x_spec,
        out_shape=jax.ShapeDtypeStruct(x2d.shape, F32),
        scratch_shapes=[pltpu.VMEM((rows, FFN_HIDDEN), BF16)],
        compiler_params=pltpu.CompilerParams(
            dimension_semantics=("arbitrary",),
            vmem_limit_bytes=VMEM_LIMIT_BYTES),
        name=name,
    )(x2d, *consts)


def _block_diag(z, same_head):
    tiled = jnp.concatenate([z] * HEADS_PER_QUAD, axis=0)
    return jnp.where(same_head, tiled, jnp.zeros_like(tiled))


def _interleave(*stages):
    live = list(stages)
    while live:
        for stage in list(live):
            try:
                next(stage)
            except StopIteration:
                live.remove(stage)


def _head_sum(z, ones_bd):
    n, width = z.shape
    zb = z.astype(BF16)
    stacked = jnp.concatenate([zb[:, q * QUAD:(q + 1) * QUAD] for q in range(width // QUAD)], axis=0)
    sums = _dot(stacked, ones_bd)
    return jnp.concatenate([sums[q * n:(q + 1) * n] for q in range(width // QUAD)], axis=1)


def _rwkv_kernel(x_ref, mu_ref, wr_ref, wk_ref, wv_ref, w0_ref, ww1_ref, ww2_ref, a0_ref,
                 aw1_ref, aw2_ref, gw1_ref, gw2_ref, kk_ref, ka_ref, rk_ref, gng_ref, gnb_ref,
                 wout_ref, lng_ref, lnb_ref, o_ref,
                 xlast_ref, state_ref, r_s, k_s, v_s, kk_s, kka_s, lw_s, g_s, y_s):
    rows = RWKV_ROWS
    chunk = RWKV_CHUNK
    d = D_MODEL
    n_quads = d // QUAD
    step = pl.program_id(1)

    @pl.when(step == 0)
    def _():
        xlast_ref[...] = jnp.zeros(xlast_ref.shape, F32)
        state_ref[...] = jnp.zeros(state_ref.shape, F32)

    x = x_ref[0]
    row_id = lax.broadcasted_iota(jnp.int32, (rows, d), 0)
    x_prev = jnp.where(row_id == 0, xlast_ref[0:1, :], pltpu.roll(x, 1, axis=0))
    xlast_ref[0:1, :] = x[rows - 1:rows, :]
    xx = x_prev - x

    def mixed(j):
        return (x + xx * mu_ref[j:j + 1, :]).astype(BF16)

    r_i = lax.broadcasted_iota(jnp.int32, (QUAD, QUAD), 0) // HEAD
    c_i = lax.broadcasted_iota(jnp.int32, (QUAD, QUAD), 1) // HEAD
    same_head = r_i == c_i
    ones_bd = jnp.where(same_head, 1.0, 0.0).astype(BF16)

    r = _dot(mixed(0), wr_ref[...])
    k = _dot(mixed(2), wk_ref[...])
    v = _dot(mixed(3), wv_ref[...])
    z_w = w0_ref[...] + _dot(jnp.tanh(_dot(mixed(1), ww1_ref[...])).astype(BF16), ww2_ref[...])
    lw_s[...] = -math.exp(-0.5) * _sigmoid(z_w)
    a = _sigmoid(a0_ref[...] + _dot(_dot(mixed(4), aw1_ref[...]).astype(BF16), aw2_ref[...]))
    kk = k * kk_ref[...]
    kk = kk / jnp.maximum(jnp.sqrt(_head_sum(kk * kk, ones_bd)), 1e-12)
    k2 = k * (1.0 + (a - 1.0) * ka_ref[...])
    g_s[...] = _dot(_sigmoid(_dot(mixed(5), gw1_ref[...])).astype(BF16), gw2_ref[...])
    r_s[...] = r
    k_s[...] = k2
    v_s[...] = v
    kk_s[...] = kk
    kka_s[...] = kk * a

    t_i = lax.broadcasted_iota(jnp.int32, (chunk, chunk), 0)
    s_i = lax.broadcasted_iota(jnp.int32, (chunk, chunk), 1)
    tri_incl = jnp.where(t_i >= s_i, 1.0, 0.0).astype(BF16)
    t_n = lax.broadcasted_iota(jnp.int32, (2 * chunk, QUAD), 0)
    s_n = lax.broadcasted_iota(jnp.int32, (2 * chunk, QUAD), 1) % HEAD
    causal2 = jnp.where(t_n < chunk, t_n, t_n - chunk + 1) > s_n
    eye_nat = jnp.where(lax.broadcasted_iota(jnp.int32, (chunk, QUAD), 0)
                        == lax.broadcasted_iota(jnp.int32, (chunk, QUAD), 1) % HEAD, 1.0, 0.0).astype(F32)

    n_chunks = rows // chunk
    quad = lambda z, q: z[:, q * QUAD:(q + 1) * QUAD]
    bd = lambda z: _block_diag(z, same_head)
    ready = {}

    def prepare(cs):
        items = [(c, q) for c in cs for q in range(n_quads)]
        prep = {}
        for c in cs:
            rws = slice(c * chunk, (c + 1) * chunk)
            lw = lw_s[rws, :]
            lw_hi = lw.astype(BF16)
            lw_lo = (lw - lw_hi.astype(F32)).astype(BF16)
            cum = _dot(tri_incl, lw_hi) + _dot(tri_incl, lw_lo)
            tot = cum[chunk - 1:chunk, :]
            gam_inv = jnp.exp(-cum)
            gam_rest = jnp.exp(tot - cum)
            kk_c = kk_s[rws, :]
            kka_c = kka_s[rws, :]
            k_c = k_s[rws, :]
            prep[c] = dict(
                a_t=(-kk_c * jnp.exp(cum - lw)).astype(BF16),
                r_t=(r_s[rws, :] * jnp.exp(cum)).astype(BF16),
                b_t=(kka_c * gam_inv).astype(BF16),
                k_t=(k_c * gam_inv).astype(BF16),
                b_h=(kka_c * gam_rest).astype(BF16),
                k_h=(k_c * gam_rest).astype(BF16),
                gam_tot=jnp.exp(tot))
        yield
        ar = [jnp.concatenate([quad(prep[c]["a_t"], q), quad(prep[c]["r_t"], q)], axis=0) for c, q in items]
        s_b = [jnp.where(causal2, _dot_nt(ar[i], bd(quad(prep[c]["b_t"], q))), 0.0)
               for i, (c, q) in enumerate(items)]
        akrk = [jnp.where(causal2, _dot_nt(ar[i], bd(quad(prep[c]["k_t"], q))), 0.0).astype(BF16)
                for i, (c, q) in enumerate(items)]
        rb = [z[chunk:].astype(BF16) for z in s_b]
        yield
        r_pow = [z[0:chunk] for z in s_b]
        q_sum = [eye_nat + z for z in r_pow]
        r_pow = [_dot(z.astype(BF16), bd(z.astype(BF16))) for z in r_pow]
        yield
        for _ in range(int(math.log2(chunk)) - 2):
            r_b = [z.astype(BF16) for z in r_pow]
            prod = [_dot(jnp.concatenate([qs.astype(BF16), z], axis=0), bd(z)) for qs, z in zip(q_sum, r_b)]
            q_sum = [qs + p[0:chunk] for qs, p in zip(q_sum, prod)]
            r_pow = [p[chunk:] for p in prod]
            yield
        p_inv = [(qs + _dot(qs.astype(BF16), bd(z.astype(BF16)))).astype(BF16) for qs, z in zip(q_sum, r_pow)]
        for i, (c, q) in enumerate(items):
            ready[c, q] = dict(ar=ar[i], akrk=akrk[i], rb=rb[i], p_inv=p_inv[i],
                               bk=jnp.concatenate([quad(prep[c]["b_h"], q), quad(prep[c]["k_h"], q)], axis=0),
                               gam_tot=quad(prep[c]["gam_tot"], q))

    def advance(c):
        rws = slice(c * chunk, (c + 1) * chunk)
        v_c = v_s[rws, :]
        v_b = v_c.astype(BF16)
        it = [ready.pop((c, q)) for q in range(n_quads)]
        s0 = [state_ref[q] for q in range(n_quads)]
        sv = [_dot_nt(it[q]["ar"], s0[q].astype(BF16)) + _dot(it[q]["akrk"], bd(quad(v_b, q)))
              for q in range(n_quads)]
        yield
        u = [_dot(it[q]["p_inv"], bd(sv[q][0:chunk].astype(BF16))) for q in range(n_quads)]
        yield
        for q in range(n_quads):
            y_s[rws, q * QUAD:(q + 1) * QUAD] = sv[q][chunk:] + _dot(it[q]["rb"], bd(u[q].astype(BF16)))
        for q in range(n_quads):
            uv_t = jnp.concatenate([u[q], quad(v_c, q)], axis=0).T.astype(BF16)
            state_ref[q] = s0[q] * it[q]["gam_tot"] + jnp.where(same_head, _dot(uv_t, it[q]["bk"]), 0.0)
        yield

    def finish(cs):
        rws = slice(cs[0] * chunk, (cs[-1] + 1) * chunk)
        y = y_s[rws, :]
        y_c = y - _head_sum(y, ones_bd) * (1.0 / HEAD)
        yield
        y_var = _head_sum(y_c * y_c, ones_bd) * (1.0 / HEAD)
        bonus = _head_sum(r_s[rws, :] * k_s[rws, :] * rk_ref[...], ones_bd) * v_s[rws, :]
        yield
        y_n = y_c * lax.rsqrt(y_var + GN_EPS) * gng_ref[...] + gnb_ref[...]
        mix = _dot(((y_n + bonus) * g_s[rws, :]).astype(BF16), wout_ref[...])
        yield
        o_ref[0, rws, :] = _layer_norm(ALPHA * x_ref[0, rws, :] + mix, lng_ref[...], lnb_ref[...])

    def chain(cs):
        for c in cs:
            yield from advance(c)

    groups = [list(range(g, min(g + RWKV_GROUP, n_chunks))) for g in range(0, n_chunks, RWKV_GROUP)]
    _interleave(prepare(groups[0]))
    for g, cs in enumerate(groups):
        stages = [chain(cs)]
        if g + 1 < len(groups):
            stages.append(prepare(groups[g + 1]))
        if g > 0:
            stages.append(finish(groups[g - 1]))
        _interleave(*stages)
    _interleave(finish(groups[-1]))


def _rwkv(x, consts):
    bsz, seq, d = x.shape
    rows = RWKV_ROWS
    x_spec = pl.BlockSpec((1, rows, d), lambda b, i: (b, i, 0))
    tile = pltpu.VMEM((rows, d), F32)
    return pl.pallas_call(
        _rwkv_kernel,
        grid=(bsz, seq // rows),
        in_specs=[x_spec] + [_const_spec(c.shape) for c in consts],
        out_specs=x_spec,
        out_shape=jax.ShapeDtypeStruct(x.shape, F32),
        scratch_shapes=[
            pltpu.VMEM((8, d), F32),
            pltpu.VMEM((d // QUAD, QUAD, QUAD), F32),
            tile, tile, tile, tile, tile, tile, tile, tile,
        ],
        compiler_params=pltpu.CompilerParams(
            dimension_semantics=("arbitrary", "arbitrary"),
            vmem_limit_bytes=VMEM_LIMIT_BYTES),
        name="rwkv7",
    )(x, *consts)


def _row(p):
    return p.reshape(1, -1).astype(F32)


def kernel(x, ab_w_in, ab_b_in, conv_w, conv_b, conv_norm_g, conv_norm_b, sgu_norm_g, sgu_norm_b, sgu_w, sgu_b, ab_w_out, ab_b_out, rwkv_mu, rwkv_w_rkv, rwkv_w0, rwkv_w_w1, rwkv_w_w2, rwkv_a0, rwkv_a_w1, rwkv_a_w2, rwkv_g_w1, rwkv_g_w2, rwkv_k_k, rwkv_k_a, rwkv_r_k, rwkv_ln_g, rwkv_ln_b, rwkv_w_out, ffn_w_in, ffn_w_out, ln_mix_g, ln_mix_b, ln_ffn_g, ln_ffn_b):
    bsz, seq, d = x.shape
    for layer in range(DEPTH):
        i = layer // 2
        if layer % 2 == 0:
            sgu_b_full = jnp.broadcast_to(sgu_b[i][:, :, None], (SGU_GROUPS, SGU_CHUNK, SGU_GROUP_DIM))
            x = _mixer0(
                x, ab_w_in[i].astype(BF16), _row(ab_b_in[i]),
                jnp.broadcast_to(conv_w[i].reshape(CONV_WIDTH, 1, CONV_DIM),
                                 (CONV_WIDTH, CONV_ROW_BLOCK, CONV_DIM)), _row(conv_b[i]),
                _row(conv_norm_g[i]), _row(conv_norm_b[i]), _row(sgu_norm_g[i]), _row(sgu_norm_b[i]),
                sgu_w[i], sgu_b_full, ab_w_out[i].astype(BF16), _row(ab_b_out[i]),
                _row(ln_mix_g[layer]), _row(ln_mix_b[layer]))
        else:
            consts = (
                rwkv_mu[i], rwkv_w_rkv[i, 0].astype(BF16), rwkv_w_rkv[i, 1].astype(BF16),
                rwkv_w_rkv[i, 2].astype(BF16), _row(rwkv_w0[i]), rwkv_w_w1[i].astype(BF16),
                rwkv_w_w2[i].astype(BF16), _row(rwkv_a0[i]), rwkv_a_w1[i].astype(BF16),
                rwkv_a_w2[i].astype(BF16), rwkv_g_w1[i].astype(BF16), rwkv_g_w2[i].astype(BF16),
                _row(rwkv_k_k[i]), _row(rwkv_k_a[i]), _row(rwkv_r_k[i]), _row(rwkv_ln_g[i]),
                _row(rwkv_ln_b[i]), rwkv_w_out[i].astype(BF16),
                _row(ln_mix_g[layer]), _row(ln_mix_b[layer]))
            x = _rwkv(x, consts)
        x = _ffn(x.reshape(bsz * seq, d), ffn_w_in[layer].astype(BF16), ffn_w_out[layer].astype(BF16),
                 _row(ln_ffn_g[layer]), _row(ln_ffn_b[layer]), name=f"ffn{layer}").reshape(bsz, seq, d)
    return x
```

```python
import math

import jax
import jax.numpy as jnp
from jax import lax
from jax.experimental import pallas as pl
from jax.experimental.pallas import tpu as pltpu

F32 = jnp.float32
BF16 = jnp.bfloat16

D_MODEL = 1024
DEPTH = 2
ALPHA = (2.0 * DEPTH) ** 0.25
LN_EPS = 1e-5

CONV_DIM = 512
CONV_WIDTH = 31
SGU_DIM = 512
SGU_GROUPS = 4
SGU_GROUP_DIM = SGU_DIM // SGU_GROUPS
SGU_CHUNK = 128

HEAD = 64
GN_EPS = 64e-5
FFN_HIDDEN = 2816

LANES = 128
SUBLANES = 8
MXU_DIM = 256
VMEM_LIMIT_BYTES = 56 * 1024 * 1024

TILE_ROWS = 256
CONV_HALO = 32
CONV_ROW_BLOCK = 32
FFN_COLS = 256
RWKV_CHUNK = 64
RWKV_GROUP = 2
QUAD = MXU_DIM
HEADS_PER_QUAD = QUAD // HEAD


def _dot(a, b):
    return jnp.dot(a, b, preferred_element_type=F32)


def _dot_nt(a, b):
    return lax.dot_general(a, b, (((1,), (1,)), ((), ())), preferred_element_type=F32)


def _layer_norm(x, g, b, eps=LN_EPS):
    mu = jnp.mean(x, axis=-1, keepdims=True)
    xc = x - mu
    var = jnp.mean(xc * xc, axis=-1, keepdims=True)
    return xc * lax.rsqrt(var + eps) * g + b


def _sigmoid(x):
    return 1.0 / (1.0 + jnp.exp(-x))


def _gelu_tanh(x):
    c = math.sqrt(2.0 / math.pi)
    return 0.5 * x * (1.0 + jnp.tanh(c * (x + 0.044715 * (x * x * x))))


def _const_spec(shape):
    nd = len(shape)
    return pl.BlockSpec(shape, lambda *_: (0,) * nd, pipeline_mode=pl.Buffered(1))


class _Background:
    def __init__(self, steps, pace):
        self._steps = steps
        self._pace = pace
        self._credit = 0.0

    def tick(self):
        self._credit += self._pace
        while self._credit >= 1.0:
            self._credit -= 1.0
            next(self._steps, None)

    def drain(self):
        for _ in self._steps:
            pass


def _interleave(background, *stages):
    live = list(stages)
    while live:
        for stage in list(live):
            try:
                next(stage)
            except StopIteration:
                live.remove(stage)
        background.tick()


def _ffn_steps(x1_ref, x1b_ref, act_ref, acc_ref, win_ref, wout_ref, g_ref, b_ref, o_ref):
    for c in range(FFN_HIDDEN // FFN_COLS):
        lo = c * FFN_COLS
        xb = x1b_ref[1]
        gate = _dot(xb, win_ref[:, lo:lo + FFN_COLS])
        up = _dot(xb, win_ref[:, FFN_HIDDEN + lo:FFN_HIDDEN + lo + FFN_COLS])
        act_ref[:, lo:lo + FFN_COLS] = (gate * _sigmoid(gate) * up).astype(BF16)
        yield
    for n in range(D_MODEL // FFN_COLS):
        lo = n * FFN_COLS
        acc_ref[:, lo:lo + FFN_COLS] = _dot(act_ref[...], wout_ref[:, lo:lo + FFN_COLS])
        yield
    o_ref[0] = _layer_norm(ALPHA * x1_ref[1] + acc_ref[...], g_ref[...], b_ref[...])
    yield


def _ffn_scratch(rows):
    return [
        pltpu.VMEM((2, rows, D_MODEL), F32),
        pltpu.VMEM((2, rows, D_MODEL), BF16),
        pltpu.VMEM((rows, FFN_HIDDEN), BF16),
        pltpu.VMEM((rows, D_MODEL), F32),
    ]


def _layer_call(kernel_fn, x, consts, scratch, name):
    bsz, seq, d = x.shape
    rows = TILE_ROWS
    per_seq = seq // rows
    n_tiles = bsz * per_seq

    def in_map(t):
        tile = jnp.minimum(t, n_tiles - 1)
        return (tile // per_seq, tile % per_seq, 0)

    def out_map(t):
        tile = jnp.maximum(t - 1, 0)
        return (tile // per_seq, tile % per_seq, 0)

    return pl.pallas_call(
        kernel_fn,
        grid=(n_tiles + 1,),
        in_specs=[pl.BlockSpec((1, rows, d), in_map)] + [_const_spec(c.shape) for c in consts],
        out_specs=pl.BlockSpec((1, rows, d), out_map),
        out_shape=jax.ShapeDtypeStruct(x.shape, F32),
        scratch_shapes=_ffn_scratch(rows) + scratch,
        compiler_params=pltpu.CompilerParams(
            dimension_semantics=("arbitrary",),
            vmem_limit_bytes=VMEM_LIMIT_BYTES),
        name=name,
    )(x, *consts)


def _layer0_kernel(x_ref, win_ref, bin_ref, cw_ref, cb_ref, cng_ref, cnb_ref, sng_ref, snb_ref,
                   ws_ref, sb_ref, wout_ref, bout_ref, lng_ref, lnb_ref,
                   fwin_ref, fwout_ref, fg_ref, fb_ref, o_ref,
                   x1_ref, x1b_ref, act_ref, acc_ref, ybuf_ref, yrot_ref, h_ref, mixed_ref,
                   *, tiles_per_seq, n_tiles):
    rows = TILE_ROWS
    blk_rows = SGU_CHUNK
    n_blocks = rows // blk_rows
    step = pl.program_id(0)

    @pl.when(step == 0)
    def _():
        x1_ref[0] = jnp.zeros((rows, D_MODEL), F32)
        x1b_ref[0] = jnp.zeros((rows, D_MODEL), BF16)

    x1_ref[1] = x1_ref[0]
    x1b_ref[1] = x1b_ref[0]

    @pl.when(jnp.minimum(step, n_tiles - 1) % tiles_per_seq == 0)
    def _():
        ybuf_ref[0:CONV_HALO, :] = jnp.zeros((CONV_HALO, CONV_DIM), F32)

    t_idx = lax.broadcasted_iota(jnp.int32, (SGU_CHUNK, SGU_CHUNK), 0)
    s_idx = lax.broadcasted_iota(jnp.int32, (SGU_CHUNK, SGU_CHUNK), 1)
    w_s = [jnp.where(t_idx >= s_idx, ws_ref[g], 0.0).astype(BF16) for g in range(SGU_GROUPS)]
    first = CONV_HALO - (CONV_WIDTH - 1)
    reach = blk_rows + CONV_HALO - SUBLANES

    def project(b):
        rws = slice(b * blk_rows, (b + 1) * blk_rows)
        h_ref[rws, :] = _dot(x_ref[0, rws, :].astype(BF16), win_ref[...]) + bin_ref[...]
        yield

    def mix_block(b):
        r0 = b * blk_rows
        rws = slice(r0, r0 + blk_rows)
        ybuf_ref[CONV_HALO + r0:CONV_HALO + r0 + blk_rows, :] = (
            h_ref[rws, 0:CONV_DIM] * _sigmoid(h_ref[rws, CONV_DIM:2 * CONV_DIM]))
        for r in range(1, SUBLANES):
            yrot_ref[r - 1] = ybuf_ref[r0 + r:r0 + reach + r, :]
        yield
        for sub in range(blk_rows // CONV_ROW_BLOCK):
            base = sub * CONV_ROW_BLOCK
            acc = jnp.broadcast_to(cb_ref[...], (CONV_ROW_BLOCK, CONV_DIM))
            for j in range(CONV_WIDTH):
                r = (first + j) % SUBLANES
                start = base + first + j - r
                if r == 0:
                    window = ybuf_ref[r0 + start:r0 + start + CONV_ROW_BLOCK, :]
                else:
                    window = yrot_ref[r - 1, start:start + CONV_ROW_BLOCK, :]
                acc = acc + cw_ref[j] * window
            ya = _layer_norm(acc, cng_ref[...], cnb_ref[...])
            mixed_ref[r0 + base:r0 + base + CONV_ROW_BLOCK, 0:CONV_DIM] = (ya * _sigmoid(ya)).astype(BF16)
            yield
        u = _gelu_tanh(h_ref[rws, 2 * CONV_DIM:2 * CONV_DIM + SGU_DIM])
        v = _layer_norm(_gelu_tanh(h_ref[rws, 2 * CONV_DIM + SGU_DIM:]), sng_ref[...], snb_ref[...]).astype(BF16)
        for g in range(SGU_GROUPS):
            cols = slice(g * SGU_GROUP_DIM, (g + 1) * SGU_GROUP_DIM)
            sv = _dot(w_s[g], v[:, cols]) + sb_ref[g]
            mixed_ref[rws, CONV_DIM + g * SGU_GROUP_DIM:CONV_DIM + (g + 1) * SGU_GROUP_DIM] = (
                u[:, cols] * sv).astype(BF16)
        yield
        mix = _dot(mixed_ref[rws, :], wout_ref[...]) + bout_ref[...]
        x1 = _layer_norm(ALPHA * x_ref[0, rws, :] + mix, lng_ref[...], lnb_ref[...])
        x1_ref[0, rws, :] = x1
        x1b_ref[0, rws, :] = x1.astype(BF16)
        yield

    ffn = _Background(_ffn_steps(x1_ref, x1b_ref, act_ref, acc_ref, fwin_ref, fwout_ref,
                                 fg_ref, fb_ref, o_ref), pace=1.0)
    _interleave(ffn, project(0))
    for b in range(n_blocks):
        stages = [mix_block(b)]
        if b + 1 < n_blocks:
            stages.append(project(b + 1))
        _interleave(ffn, *stages)
    ffn.drain()
    ybuf_ref[0:CONV_HALO, :] = ybuf_ref[rows:rows + CONV_HALO, :]


def _layer0(x, mixer_consts, ffn_consts):
    bsz, seq, _ = x.shape
    rows = TILE_ROWS
    scratch = [
        pltpu.VMEM((CONV_HALO + rows, CONV_DIM), F32),
        pltpu.VMEM((SUBLANES - 1, SGU_CHUNK + CONV_HALO - SUBLANES, CONV_DIM), F32),
        pltpu.VMEM((rows, 2 * CONV_DIM + 2 * SGU_DIM), F32),
        pltpu.VMEM((rows, CONV_DIM + SGU_DIM), BF16),
    ]
    kernel_fn = lambda *refs: _layer0_kernel(*refs, tiles_per_seq=seq // rows, n_tiles=bsz * seq // rows)
    return _layer_call(kernel_fn, x, tuple(mixer_consts) + tuple(ffn_consts), scratch, "layer0")


def _block_diag(z, same_head):
    tiled = jnp.concatenate([z] * HEADS_PER_QUAD, axis=0)
    return jnp.where(same_head, tiled, jnp.zeros_like(tiled))


def _head_sum(z, ones_bd):
    n, width = z.shape
    zb = z.astype(BF16)
    stacked = jnp.concatenate([zb[:, q * QUAD:(q + 1) * QUAD] for q in range(width // QUAD)], axis=0)
    sums = _dot(stacked, ones_bd)
    return jnp.concatenate([sums[q * n:(q + 1) * n] for q in range(width // QUAD)], axis=1)


def _layer1_kernel(x_ref, mu_ref, wr_ref, wk_ref, wv_ref, w0_ref, ww1_ref, ww2_ref, a0_ref,
                   aw1_ref, aw2_ref, gw1_ref, gw2_ref, kk_ref, ka_ref, rk_ref, gng_ref, gnb_ref,
                   wout_ref, lng_ref, lnb_ref,
                   fwin_ref, fwout_ref, fg_ref, fb_ref, o_ref,
                   x1_ref, x1b_ref, act_ref, acc_ref,
                   xlast_ref, state_ref, r_s, k_s, v_s, kk_s, kka_s, lw_s, g_s, y_s,
                   *, tiles_per_seq, n_tiles):
    rows = TILE_ROWS
    chunk = RWKV_CHUNK
    d = D_MODEL
    n_quads = d // QUAD
    step = pl.program_id(0)

    @pl.when(step == 0)
    def _():
        x1_ref[0] = jnp.zeros((rows, D_MODEL), F32)
        x1b_ref[0] = jnp.zeros((rows, D_MODEL), BF16)

    x1_ref[1] = x1_ref[0]
    x1b_ref[1] = x1b_ref[0]

    @pl.when(jnp.minimum(step, n_tiles - 1) % tiles_per_seq == 0)
    def _():
        xlast_ref[...] = jnp.zeros(xlast_ref.shape, F32)
        state_ref[...] = jnp.zeros(state_ref.shape, F32)

    x = x_ref[0]
    row_id = lax.broadcasted_iota(jnp.int32, (rows, d), 0)
    x_prev = jnp.where(row_id == 0, xlast_ref[0:1, :], pltpu.roll(x, 1, axis=0))
    xlast_ref[0:1, :] = x[rows - 1:rows, :]
    xx = x_prev - x

    def mixed(j):
        return (x + xx * mu_ref[j:j + 1, :]).astype(BF16)

    r_i = lax.broadcasted_iota(jnp.int32, (QUAD, QUAD), 0) // HEAD
    c_i = lax.broadcasted_iota(jnp.int32, (QUAD, QUAD), 1) // HEAD
    same_head = r_i == c_i
    ones_bd = jnp.where(same_head, 1.0, 0.0).astype(BF16)

    r = _dot(mixed(0), wr_ref[...])
    k = _dot(mixed(2), wk_ref[...])
    v = _dot(mixed(3), wv_ref[...])
    z_w = w0_ref[...] + _dot(jnp.tanh(_dot(mixed(1), ww1_ref[...])).astype(BF16), ww2_ref[...])
    lw_s[...] = -math.exp(-0.5) * _sigmoid(z_w)
    a = _sigmoid(a0_ref[...] + _dot(_dot(mixed(4), aw1_ref[...]).astype(BF16), aw2_ref[...]))
    kk = k * kk_ref[...]
    kk = kk / jnp.maximum(jnp.sqrt(_head_sum(kk * kk, ones_bd)), 1e-12)
    k2 = k * (1.0 + (a - 1.0) * ka_ref[...])
    g_s[...] = _dot(_sigmoid(_dot(mixed(5), gw1_ref[...])).astype(BF16), gw2_ref[...])
    r_s[...] = r
    k_s[...] = k2
    v_s[...] = v
    kk_s[...] = kk
    kka_s[...] = kk * a

    t_i = lax.broadcasted_iota(jnp.int32, (chunk, chunk), 0)
    s_i = lax.broadcasted_iota(jnp.int32, (chunk, chunk), 1)
    tri_incl = jnp.where(t_i >= s_i, 1.0, 0.0).astype(BF16)
    t_n = lax.broadcasted_iota(jnp.int32, (2 * chunk, QUAD), 0)
    s_n = lax.broadcasted_iota(jnp.int32, (2 * chunk, QUAD), 1) % HEAD
    causal2 = jnp.where(t_n < chunk, t_n, t_n - chunk + 1) > s_n
    eye_nat = jnp.where(lax.broadcasted_iota(jnp.int32, (chunk, QUAD), 0)
                        == lax.broadcasted_iota(jnp.int32, (chunk, QUAD), 1) % HEAD, 1.0, 0.0).astype(F32)

    n_chunks = rows // chunk
    quad = lambda z, q: z[:, q * QUAD:(q + 1) * QUAD]
    bd = lambda z: _block_diag(z, same_head)
    ready = {}

    def prepare(cs):
        items = [(c, q) for c in cs for q in range(n_quads)]
        prep = {}
        for c in cs:
            rws = slice(c * chunk, (c + 1) * chunk)
            lw = lw_s[rws, :]
            lw_hi = lw.astype(BF16)
            lw_lo = (lw - lw_hi.astype(F32)).astype(BF16)
            cum = _dot(tri_incl, lw_hi) + _dot(tri_incl, lw_lo)
            tot = cum[chunk - 1:chunk, :]
            gam_inv = jnp.exp(-cum)
            gam_rest = jnp.exp(tot - cum)
            kk_c = kk_s[rws, :]
            kka_c = kka_s[rws, :]
            k_c = k_s[rws, :]
            prep[c] = dict(
                a_t=(-kk_c * jnp.exp(cum - lw)).astype(BF16),
                r_t=(r_s[rws, :] * jnp.exp(cum)).astype(BF16),
                b_t=(kka_c * gam_inv).astype(BF16),
                k_t=(k_c * gam_inv).astype(BF16),
                b_h=(kka_c * gam_rest).astype(BF16),
                k_h=(k_c * gam_rest).astype(BF16),
                gam_tot=jnp.exp(tot))
        yield
        ar = [jnp.concatenate([quad(prep[c]["a_t"], q), quad(prep[c]["r_t"], q)], axis=0) for c, q in items]
        s_b = [jnp.where(causal2, _dot_nt(ar[i], bd(quad(prep[c]["b_t"], q))), 0.0)
               for i, (c, q) in enumerate(items)]
        akrk = [jnp.where(causal2, _dot_nt(ar[i], bd(quad(prep[c]["k_t"], q))), 0.0).astype(BF16)
                for i, (c, q) in enumerate(items)]
        rb = [z[chunk:].astype(BF16) for z in s_b]
        yield
        r_pow = [z[0:chunk] for z in s_b]
        q_sum = [eye_nat + z for z in r_pow]
        r_pow = [_dot(z.astype(BF16), bd(z.astype(BF16))) for z in r_pow]
        yield
        for _ in range(int(math.log2(chunk)) - 2):
            r_b = [z.astype(BF16) for z in r_pow]
            prod = [_dot(jnp.concatenate([qs.astype(BF16), z], axis=0), bd(z)) for qs, z in zip(q_sum, r_b)]
            q_sum = [qs + p[0:chunk] for qs, p in zip(q_sum, prod)]
            r_pow = [p[chunk:] for p in prod]
            yield
        p_inv = [(qs + _dot(qs.astype(BF16), bd(z.astype(BF16)))).astype(BF16) for qs, z in zip(q_sum, r_pow)]
        for i, (c, q) in enumerate(items):
            ready[c, q] = dict(ar=ar[i], akrk=akrk[i], rb=rb[i], p_inv=p_inv[i],
                               bk=jnp.concatenate([quad(prep[c]["b_h"], q), quad(prep[c]["k_h"], q)], axis=0),
                               gam_tot=quad(prep[c]["gam_tot"], q))

    def advance(c):
        rws = slice(c * chunk, (c + 1) * chunk)
        v_c = v_s[rws, :]
        v_b = v_c.astype(BF16)
        it = [ready.pop((c, q)) for q in range(n_quads)]
        s0 = [state_ref[q] for q in range(n_quads)]
        sv = [_dot_nt(it[q]["ar"], s0[q].astype(BF16)) + _dot(it[q]["akrk"], bd(quad(v_b, q)))
              for q in range(n_quads)]
        yield
        u = [_dot(it[q]["p_inv"], bd(sv[q][0:chunk].astype(BF16))) for q in range(n_quads)]
        yield
        for q in range(n_quads):
            y_s[rws, q * QUAD:(q + 1) * QUAD] = sv[q][chunk:] + _dot(it[q]["rb"], bd(u[q].astype(BF16)))
        for q in range(n_quads):
            uv_t = jnp.concatenate([u[q], quad(v_c, q)], axis=0).T.astype(BF16)
            state_ref[q] = s0[q] * it[q]["gam_tot"] + jnp.where(same_head, _dot(uv_t, it[q]["bk"]), 0.0)
        yield

    def finish(cs):
        rws = slice(cs[0] * chunk, (cs[-1] + 1) * chunk)
        y = y_s[rws, :]
        y_c = y - _head_sum(y, ones_bd) * (1.0 / HEAD)
        yield
        y_var = _head_sum(y_c * y_c, ones_bd) * (1.0 / HEAD)
        bonus = _head_sum(r_s[rws, :] * k_s[rws, :] * rk_ref[...], ones_bd) * v_s[rws, :]
        yield
        y_n = y_c * lax.rsqrt(y_var + GN_EPS) * gng_ref[...] + gnb_ref[...]
        mix = _dot(((y_n + bonus) * g_s[rws, :]).astype(BF16), wout_ref[...])
        yield
        x1 = _layer_norm(ALPHA * x_ref[0, rws, :] + mix, lng_ref[...], lnb_ref[...])
        x1_ref[0, rws, :] = x1
        x1b_ref[0, rws, :] = x1.astype(BF16)

    def chain(cs):
        for c in cs:
            yield from advance(c)

    ffn = _Background(_ffn_steps(x1_ref, x1b_ref, act_ref, acc_ref, fwin_ref, fwout_ref,
                                 fg_ref, fb_ref, o_ref), pace=0.65)
    groups = [list(range(g, min(g + RWKV_GROUP, n_chunks))) for g in range(0, n_chunks, RWKV_GROUP)]
    _interleave(ffn, prepare(groups[0]))
    for g, cs in enumerate(groups):
        stages = [chain(cs)]
        if g + 1 < len(groups):
            stages.append(prepare(groups[g + 1]))
        if g > 0:
            stages.append(finish(groups[g - 1]))
        _interleave(ffn, *stages)
    _interleave(ffn, finish(groups[-1]))
    ffn.drain()


def _layer1(x, mixer_consts, ffn_consts):
    bsz, seq, d = x.shape
    rows = TILE_ROWS
    tile = pltpu.VMEM((rows, d), F32)
    scratch = [
        pltpu.VMEM((SUBLANES, d), F32),
        pltpu.VMEM((d // QUAD, QUAD, QUAD), F32),
        tile, tile, tile, tile, tile, tile, tile, tile,
    ]
    kernel_fn = lambda *refs: _layer1_kernel(*refs, tiles_per_seq=seq // rows, n_tiles=bsz * seq // rows)
    return _layer_call(kernel_fn, x, tuple(mixer_consts) + tuple(ffn_consts), scratch, "layer1")


def _row(p):
    return p.reshape(1, -1).astype(F32)


def kernel(x, ab_w_in, ab_b_in, conv_w, conv_b, conv_norm_g, conv_norm_b, sgu_norm_g, sgu_norm_b, sgu_w, sgu_b, ab_w_out, ab_b_out, rwkv_mu, rwkv_w_rkv, rwkv_w0, rwkv_w_w1, rwkv_w_w2, rwkv_a0, rwkv_a_w1, rwkv_a_w2, rwkv_g_w1, rwkv_g_w2, rwkv_k_k, rwkv_k_a, rwkv_r_k, rwkv_ln_g, rwkv_ln_b, rwkv_w_out, ffn_w_in, ffn_w_out, ln_mix_g, ln_mix_b, ln_ffn_g, ln_ffn_b):
    for layer in range(DEPTH):
        i = layer // 2
        ffn_consts = (ffn_w_in[layer].astype(BF16), ffn_w_out[layer].astype(BF16),
                      _row(ln_ffn_g[layer]), _row(ln_ffn_b[layer]))
        if layer % 2 == 0:
            mixer_consts = (
                ab_w_in[i].astype(BF16), _row(ab_b_in[i]),
                jnp.broadcast_to(conv_w[i].reshape(CONV_WIDTH, 1, CONV_DIM),
                                 (CONV_WIDTH, CONV_ROW_BLOCK, CONV_DIM)), _row(conv_b[i]),
                _row(conv_norm_g[i]), _row(conv_norm_b[i]), _row(sgu_norm_g[i]), _row(sgu_norm_b[i]),
                sgu_w[i],
                jnp.broadcast_to(sgu_b[i][:, :, None], (SGU_GROUPS, SGU_CHUNK, SGU_GROUP_DIM)),
                ab_w_out[i].astype(BF16), _row(ab_b_out[i]),
                _row(ln_mix_g[layer]), _row(ln_mix_b[layer]))
            x = _layer0(x, mixer_consts, ffn_consts)
        else:
            mixer_consts = (
                rwkv_mu[i], rwkv_w_rkv[i, 0].astype(BF16), rwkv_w_rkv[i, 1].astype(BF16),
                rwkv_w_rkv[i, 2].astype(BF16), _row(rwkv_w0[i]), rwkv_w_w1[i].astype(BF16),
                rwkv_w_w2[i].astype(BF16), _row(rwkv_a0[i]), rwkv_a_w1[i].astype(BF16),
                rwkv_a_w2[i].astype(BF16), rwkv_g_w1[i].astype(BF16), rwkv_g_w2[i].astype(BF16),
                _row(rwkv_k_k[i]), _row(rwkv_k_a[i]), _row(rwkv_r_k[i]), _row(rwkv_ln_g[i]),
                _row(rwkv_ln_b[i]), rwkv_w_out[i].astype(BF16),
                _row(ln_mix_g[layer]), _row(ln_mix_b[layer]))
            x = _layer1(x, mixer_consts, ffn_consts)
    return x
```

```python
import math

import jax
import jax.numpy as jnp
from jax import lax
from jax.experimental import pallas as pl
from jax.experimental.pallas import tpu as pltpu

F32 = jnp.float32
BF16 = jnp.bfloat16

D_MODEL = 1024
DEPTH = 2
ALPHA = (2.0 * DEPTH) ** 0.25
LN_EPS = 1e-5

CONV_DIM = 512
CONV_WIDTH = 31
SGU_DIM = 512
SGU_GROUPS = 4
SGU_GROUP_DIM = SGU_DIM // SGU_GROUPS
SGU_CHUNK = 128

HEAD = 64
GN_EPS = 64e-5
FFN_HIDDEN = 2816

LANES = 128
SUBLANES = 8
MXU_DIM = 256
VMEM_LIMIT_BYTES = 56 * 1024 * 1024

MIX0_ROWS = 512
CONV_HALO = 32
CONV_ROW_BLOCK = 32
FFN_ROWS = 512
FFN_COLS = 256
RWKV_ROWS = 256
RWKV_CHUNK = 64
RWKV_GROUP = 2
RWKV_WEIGHT_SLOTS = (1, 3, 4, 6, 7, 8, 9, 15)
QUAD = MXU_DIM
HEADS_PER_QUAD = QUAD // HEAD


def _dot(a, b):
    return jnp.dot(a, b, preferred_element_type=F32)


def _dot_nt(a, b):
    return lax.dot_general(a, b, (((1,), (1,)), ((), ())), preferred_element_type=F32)


def _layer_norm(x, g, b, eps=LN_EPS):
    mu = jnp.mean(x, axis=-1, keepdims=True)
    xc = x - mu
    var = jnp.mean(xc * xc, axis=-1, keepdims=True)
    return xc * lax.rsqrt(var + eps) * g + b


def _sigmoid(x):
    return 1.0 / (1.0 + jnp.exp(-x))


def _gelu_tanh(x):
    c = math.sqrt(2.0 / math.pi)
    return 0.5 * x * (1.0 + jnp.tanh(c * (x + 0.044715 * (x * x * x))))


def _const_spec(shape):
    nd = len(shape)
    return pl.BlockSpec(shape, lambda *_: (0,) * nd, pipeline_mode=pl.Buffered(1))


def _mixer0_kernel(x_ref, win_ref, bin_ref, cw_ref, cb_ref, cng_ref, cnb_ref, sng_ref, snb_ref,
                   ws_ref, sb_ref, wout_ref, bout_ref, lng_ref, lnb_ref, o_ref,
                   win_b, wout_b, ybuf_ref, yrot_ref, h_ref, mixed_ref):
    rows = MIX0_ROWS
    blk_rows = SGU_CHUNK
    n_blocks = rows // blk_rows
    step = pl.program_id(1)

    @pl.when((pl.program_id(0) == 0) & (step == 0))
    def _():
        win_b[...] = win_ref[...].astype(BF16)
        wout_b[...] = wout_ref[...].astype(BF16)

    @pl.when(step == 0)
    def _():
        ybuf_ref[0:CONV_HALO, :] = jnp.zeros((CONV_HALO, CONV_DIM), F32)

    t_idx = lax.broadcasted_iota(jnp.int32, (SGU_CHUNK, SGU_CHUNK), 0)
    s_idx = lax.broadcasted_iota(jnp.int32, (SGU_CHUNK, SGU_CHUNK), 1)
    w_s = [jnp.where(t_idx >= s_idx, ws_ref[g], 0.0).astype(BF16) for g in range(SGU_GROUPS)]
    first = CONV_HALO - (CONV_WIDTH - 1)
    reach = blk_rows + CONV_HALO - SUBLANES

    def project(b):
        rws = slice(b * blk_rows, (b + 1) * blk_rows)
        h_ref[rws, :] = _dot(x_ref[0, rws, :].astype(BF16), win_b[...]) + bin_ref[...]

    def mix_block(b):
        r0 = b * blk_rows
        rws = slice(r0, r0 + blk_rows)
        ybuf_ref[CONV_HALO + r0:CONV_HALO + r0 + blk_rows, :] = (
            h_ref[rws, 0:CONV_DIM] * _sigmoid(h_ref[rws, CONV_DIM:2 * CONV_DIM]))
        lo = 0 if b == 0 else r0 + reach - blk_rows
        for r in range(1, SUBLANES):
            yrot_ref[r - 1, lo:r0 + reach, :] = ybuf_ref[lo + r:r0 + reach + r, :]
        for sub in range(blk_rows // CONV_ROW_BLOCK):
            base = r0 + sub * CONV_ROW_BLOCK
            acc = jnp.broadcast_to(cb_ref[...], (CONV_ROW_BLOCK, CONV_DIM))
            for j in range(CONV_WIDTH):
                r = (first + j) % SUBLANES
                src = ybuf_ref if r == 0 else yrot_ref.at[r - 1]
                start = base + first + j - r
                acc = acc + cw_ref[j] * src[start:start + CONV_ROW_BLOCK, :]
            ya = _layer_norm(acc, cng_ref[...], cnb_ref[...])
            mixed_ref[base:base + CONV_ROW_BLOCK, 0:CONV_DIM] = (ya * _sigmoid(ya)).astype(BF16)
        u = _gelu_tanh(h_ref[rws, 2 * CONV_DIM:2 * CONV_DIM + SGU_DIM])
        v = _layer_norm(_gelu_tanh(h_ref[rws, 2 * CONV_DIM + SGU_DIM:]), sng_ref[...], snb_ref[...]).astype(BF16)
        for g in range(SGU_GROUPS):
            cols = slice(g * SGU_GROUP_DIM, (g + 1) * SGU_GROUP_DIM)
            sv = _dot(w_s[g], v[:, cols]) + sb_ref[g]
            mixed_ref[rws, CONV_DIM + g * SGU_GROUP_DIM:CONV_DIM + (g + 1) * SGU_GROUP_DIM] = (
                u[:, cols] * sv).astype(BF16)
        mix = _dot(mixed_ref[rws, :], wout_b[...]) + bout_ref[...]
        o_ref[0, rws, :] = _layer_norm(ALPHA * x_ref[0, rws, :] + mix, lng_ref[...], lnb_ref[...])

    project(0)
    for b in range(n_blocks):
        if b + 1 < n_blocks:
            project(b + 1)
        mix_block(b)
    ybuf_ref[0:CONV_HALO, :] = ybuf_ref[rows:rows + CONV_HALO, :]


def _mixer0(x, w_in, b_in, conv_w, conv_b, cn_g, cn_b, sn_g, sn_b, sgu_w, sgu_b_full,
            w_out, b_out, ln_g, ln_b):
    bsz, seq, d = x.shape
    rows = MIX0_ROWS
    x_spec = pl.BlockSpec((1, rows, d), lambda b, i: (b, i, 0))
    consts = (w_in, b_in, conv_w, conv_b, cn_g, cn_b, sn_g, sn_b, sgu_w, sgu_b_full,
              w_out, b_out, ln_g, ln_b)
    return pl.pallas_call(
        _mixer0_kernel,
        grid=(bsz, seq // rows),
        in_specs=[x_spec] + [_const_spec(c.shape) for c in consts],
        out_specs=x_spec,
        out_shape=jax.ShapeDtypeStruct(x.shape, F32),
        scratch_shapes=[
            pltpu.VMEM(w_in.shape, BF16),
            pltpu.VMEM(w_out.shape, BF16),
            pltpu.VMEM((CONV_HALO + rows, CONV_DIM), F32),
            pltpu.VMEM((SUBLANES - 1, CONV_HALO + rows - SUBLANES, CONV_DIM), F32),
            pltpu.VMEM((rows, 2 * CONV_DIM + 2 * SGU_DIM), F32),
            pltpu.VMEM((rows, CONV_DIM + SGU_DIM), BF16),
        ],
        compiler_params=pltpu.CompilerParams(
            dimension_semantics=("arbitrary", "arbitrary"),
            vmem_limit_bytes=VMEM_LIMIT_BYTES),
        name="mixer0",
    )(x, *consts)


def _ffn_kernel(x_ref, win_ref, wout_ref, g_ref, b_ref, o_ref, act_ref):
    x = x_ref[...]
    xb = x.astype(BF16)
    for c in range(FFN_HIDDEN // FFN_COLS):
        lo = c * FFN_COLS
        gate = _dot(xb, win_ref[0, :, lo:lo + FFN_COLS].astype(BF16))
        up = _dot(xb, win_ref[0, :, FFN_HIDDEN + lo:FFN_HIDDEN + lo + FFN_COLS].astype(BF16))
        act_ref[:, lo:lo + FFN_COLS] = (gate * _sigmoid(gate) * up).astype(BF16)
    y = _dot(act_ref[...], wout_ref[0].astype(BF16))
    o_ref[...] = _layer_norm(ALPHA * x + y, g_ref[...], b_ref[...])


def _ffn(x2d, w_in_all, w_out_all, ln_g, ln_b, layer, name):
    n, d = x2d.shape
    rows = FFN_ROWS
    x_spec = pl.BlockSpec((rows, d), lambda i: (i, 0))
    layer_spec = lambda w: pl.BlockSpec((1,) + w.shape[1:], lambda i: (layer, 0, 0),
                                        pipeline_mode=pl.Buffered(1))
    return pl.pallas_call(
        _ffn_kernel,
        grid=(n // rows,),
        in_specs=[x_spec, layer_spec(w_in_all), layer_spec(w_out_all),
                  _const_spec(ln_g.shape), _const_spec(ln_b.shape)],
        out_specs=x_spec,
        out_shape=jax.ShapeDtypeStruct(x2d.shape, F32),
        scratch_shapes=[pltpu.VMEM((rows, FFN_HIDDEN), BF16)],
        compiler_params=pltpu.CompilerParams(
            dimension_semantics=("arbitrary",),
            vmem_limit_bytes=VMEM_LIMIT_BYTES),
        name=name,
    )(x2d, w_in_all, w_out_all, ln_g, ln_b)


def _block_diag(z, same_head):
    tiled = jnp.concatenate([z] * HEADS_PER_QUAD, axis=0)
    return jnp.where(same_head, tiled, jnp.zeros_like(tiled))


def _interleave(*stages):
    live = list(stages)
    while live:
        for stage in list(live):
            try:
                next(stage)
            except StopIteration:
                live.remove(stage)


def _head_sum(z, ones_bd):
    n, width = z.shape
    zb = z.astype(BF16)
    stacked = jnp.concatenate([zb[:, q * QUAD:(q + 1) * QUAD] for q in range(width // QUAD)], axis=0)
    sums = _dot(stacked, ones_bd)
    return jnp.concatenate([sums[q * n:(q + 1) * n] for q in range(width // QUAD)], axis=1)


def _rwkv_kernel(x_ref, mu_ref, wrkv_ref, w0_ref, ww1_ref, ww2_ref, a0_ref,
                 aw1_ref, aw2_ref, gw1_ref, gw2_ref, kk_ref, ka_ref, rk_ref, gng_ref, gnb_ref,
                 wout_ref, lng_ref, lnb_ref, o_ref,
                 wrkv_b, ww1_b, ww2_b, aw1_b, aw2_b, gw1_b, gw2_b, wout_b,
                 xlast_ref, state_ref, r_s, k_s, v_s, kk_s, kka_s, lw_s, g_s, y_s):
    rows = RWKV_ROWS
    chunk = RWKV_CHUNK
    d = D_MODEL
    n_quads = d // QUAD
    step = pl.program_id(1)

    @pl.when((pl.program_id(0) == 0) & (step == 0))
    def _():
        for j in range(3):
            wrkv_b[j] = wrkv_ref[j].astype(BF16)
        for src, dst in ((ww1_ref, ww1_b), (ww2_ref, ww2_b), (aw1_ref, aw1_b), (aw2_ref, aw2_b),
                         (gw1_ref, gw1_b), (gw2_ref, gw2_b), (wout_ref, wout_b)):
            dst[...] = src[...].astype(BF16)

    @pl.when(step == 0)
    def _():
        xlast_ref[...] = jnp.zeros(xlast_ref.shape, F32)
        state_ref[...] = jnp.zeros(state_ref.shape, F32)

    x = x_ref[0]
    row_id = lax.broadcasted_iota(jnp.int32, (rows, d), 0)
    x_prev = jnp.where(row_id == 0, xlast_ref[0:1, :], pltpu.roll(x, 1, axis=0))
    xlast_ref[0:1, :] = x[rows - 1:rows, :]
    xx = x_prev - x

    def mixed(j):
        return (x + xx * mu_ref[j:j + 1, :]).astype(BF16)

    r_i = lax.broadcasted_iota(jnp.int32, (QUAD, QUAD), 0) // HEAD
    c_i = lax.broadcasted_iota(jnp.int32, (QUAD, QUAD), 1) // HEAD
    same_head = r_i == c_i
    ones_bd = jnp.where(same_head, 1.0, 0.0).astype(BF16)

    r = _dot(mixed(0), wrkv_b[0])
    k = _dot(mixed(2), wrkv_b[1])
    v = _dot(mixed(3), wrkv_b[2])
    z_w = w0_ref[...] + _dot(jnp.tanh(_dot(mixed(1), ww1_b[...])).astype(BF16), ww2_b[...])
    lw_s[...] = -math.exp(-0.5) * _sigmoid(z_w)
    a = _sigmoid(a0_ref[...] + _dot(_dot(mixed(4), aw1_b[...]).astype(BF16), aw2_b[...]))
    kk = k * kk_ref[...]
    kk = kk / jnp.maximum(jnp.sqrt(_head_sum(kk * kk, ones_bd)), 1e-12)
    k2 = k * (1.0 + (a - 1.0) * ka_ref[...])
    g_s[...] = _dot(_sigmoid(_dot(mixed(5), gw1_b[...])).astype(BF16), gw2_b[...])
    r_s[...] = r
    k_s[...] = k2
    v_s[...] = v
    kk_s[...] = kk
    kka_s[...] = kk * a

    t_i = lax.broadcasted_iota(jnp.int32, (chunk, chunk), 0)
    s_i = lax.broadcasted_iota(jnp.int32, (chunk, chunk), 1)
    tri_incl = jnp.where(t_i >= s_i, 1.0, 0.0).astype(BF16)
    t_n = lax.broadcasted_iota(jnp.int32, (2 * chunk, QUAD), 0)
    s_n = lax.broadcasted_iota(jnp.int32, (2 * chunk, QUAD), 1) % HEAD
    causal2 = jnp.where(t_n < chunk, t_n, t_n - chunk + 1) > s_n
    eye_nat = jnp.where(lax.broadcasted_iota(jnp.int32, (chunk, QUAD), 0)
                        == lax.broadcasted_iota(jnp.int32, (chunk, QUAD), 1) % HEAD, 1.0, 0.0).astype(F32)

    n_chunks = rows // chunk
    quad = lambda z, q: z[:, q * QUAD:(q + 1) * QUAD]
    bd = lambda z: _block_diag(z, same_head)
    ready = {}

    def prepare(cs):
        items = [(c, q) for c in cs for q in range(n_quads)]
        prep = {}
        for c in cs:
            rws = slice(c * chunk, (c + 1) * chunk)
            lw = lw_s[rws, :]
            lw_hi = lw.astype(BF16)
            lw_lo = (lw - lw_hi.astype(F32)).astype(BF16)
            cum = _dot(tri_incl, lw_hi) + _dot(tri_incl, lw_lo)
            tot = cum[chunk - 1:chunk, :]
            gam_inv = jnp.exp(-cum)
            gam_rest = jnp.exp(tot - cum)
            kk_c = kk_s[rws, :]
            kka_c = kka_s[rws, :]
            k_c = k_s[rws, :]
            prep[c] = dict(
                a_t=(-kk_c * jnp.exp(cum - lw)).astype(BF16),
                r_t=(r_s[rws, :] * jnp.exp(cum)).astype(BF16),
                b_t=(kka_c * gam_inv).astype(BF16),
                k_t=(k_c * gam_inv).astype(BF16),
                b_h=(kka_c * gam_rest).astype(BF16),
                k_h=(k_c * gam_rest).astype(BF16),
                gam_tot=jnp.exp(tot))
        yield
        ar = [jnp.concatenate([quad(prep[c]["a_t"], q), quad(prep[c]["r_t"], q)], axis=0) for c, q in items]
        s_b = [jnp.where(causal2, _dot_nt(ar[i], bd(quad(prep[c]["b_t"], q))), 0.0)
               for i, (c, q) in enumerate(items)]
        akrk = [jnp.where(causal2, _dot_nt(ar[i], bd(quad(prep[c]["k_t"], q))), 0.0).astype(BF16)
                for i, (c, q) in enumerate(items)]
        rb = [z[chunk:].astype(BF16) for z in s_b]
        yield
        r_pow = [z[0:chunk] for z in s_b]
        q_sum = [eye_nat + z for z in r_pow]
        r_pow = [_dot(z.astype(BF16), bd(z.astype(BF16))) for z in r_pow]
        yield
        for _ in range(int(math.log2(chunk)) - 2):
            r_b = [z.astype(BF16) for z in r_pow]
            prod = [_dot(jnp.concatenate([qs.astype(BF16), z], axis=0), bd(z)) for qs, z in zip(q_sum, r_b)]
            q_sum = [qs + p[0:chunk] for qs, p in zip(q_sum, prod)]
            r_pow = [p[chunk:] for p in prod]
            yield
        p_inv = [(qs + _dot(qs.astype(BF16), bd(z.astype(BF16)))).astype(BF16) for qs, z in zip(q_sum, r_pow)]
        for i, (c, q) in enumerate(items):
            ready[c, q] = dict(ar=ar[i], akrk=akrk[i], rb=rb[i], p_inv=p_inv[i],
                               bk=jnp.concatenate([quad(prep[c]["b_h"], q), quad(prep[c]["k_h"], q)], axis=0),
                               gam_tot=quad(prep[c]["gam_tot"], q))

    def advance(c):
        rws = slice(c * chunk, (c + 1) * chunk)
        v_c = v_s[rws, :]
        v_b = v_c.astype(BF16)
        it = [ready.pop((c, q)) for q in range(n_quads)]
        s0 = [state_ref[q] for q in range(n_quads)]
        sv = [_dot_nt(it[q]["ar"], s0[q].astype(BF16)) + _dot(it[q]["akrk"], bd(quad(v_b, q)))
              for q in range(n_quads)]
        yield
        u = [_dot(it[q]["p_inv"], bd(sv[q][0:chunk].astype(BF16))) for q in range(n_quads)]
        yield
        for q in range(n_quads):
            y_s[rws, q * QUAD:(q + 1) * QUAD] = sv[q][chunk:] + _dot(it[q]["rb"], bd(u[q].astype(BF16)))
        for q in range(n_quads):
            uv_t = jnp.concatenate([u[q], quad(v_c, q)], axis=0).T.astype(BF16)
            state_ref[q] = s0[q] * it[q]["gam_tot"] + jnp.where(same_head, _dot(uv_t, it[q]["bk"]), 0.0)
        yield

    def finish(cs):
        rws = slice(cs[0] * chunk, (cs[-1] + 1) * chunk)
        y = y_s[rws, :]
        y_c = y - _head_sum(y, ones_bd) * (1.0 / HEAD)
        yield
        y_var = _head_sum(y_c * y_c, ones_bd) * (1.0 / HEAD)
        bonus = _head_sum(r_s[rws, :] * k_s[rws, :] * rk_ref[...], ones_bd) * v_s[rws, :]
        yield
        y_n = y_c * lax.rsqrt(y_var + GN_EPS) * gng_ref[...] + gnb_ref[...]
        mix = _dot(((y_n + bonus) * g_s[rws, :]).astype(BF16), wout_b[...])
        yield
        o_ref[0, rws, :] = _layer_norm(ALPHA * x_ref[0, rws, :] + mix, lng_ref[...], lnb_ref[...])

    def chain(cs):
        for c in cs:
            yield from advance(c)

    groups = [list(range(g, min(g + RWKV_GROUP, n_chunks))) for g in range(0, n_chunks, RWKV_GROUP)]
    _interleave(prepare(groups[0]))
    for g, cs in enumerate(groups):
        stages = [chain(cs)]
        if g + 1 < len(groups):
            stages.append(prepare(groups[g + 1]))
        if g > 0:
            stages.append(finish(groups[g - 1]))
        _interleave(*stages)
    _interleave(finish(groups[-1]))


def _rwkv(x, consts):
    bsz, seq, d = x.shape
    rows = RWKV_ROWS
    x_spec = pl.BlockSpec((1, rows, d), lambda b, i: (b, i, 0))
    tile = pltpu.VMEM((rows, d), F32)
    return pl.pallas_call(
        _rwkv_kernel,
        grid=(bsz, seq // rows),
        in_specs=[x_spec] + [_const_spec(c.shape) for c in consts],
        out_specs=x_spec,
        out_shape=jax.ShapeDtypeStruct(x.shape, F32),
        scratch_shapes=[pltpu.VMEM(consts[j].shape, BF16) for j in RWKV_WEIGHT_SLOTS] + [
            pltpu.VMEM((SUBLANES, d), F32),
            pltpu.VMEM((d // QUAD, QUAD, QUAD), F32),
            tile, tile, tile, tile, tile, tile, tile, tile,
        ],
        compiler_params=pltpu.CompilerParams(
            dimension_semantics=("arbitrary", "arbitrary"),
            vmem_limit_bytes=VMEM_LIMIT_BYTES),
        name="rwkv7",
    )(x, *consts)


def _row(p):
    return p.reshape(1, -1).astype(F32)


def kernel(x, ab_w_in, ab_b_in, conv_w, conv_b, conv_norm_g, conv_norm_b, sgu_norm_g, sgu_norm_b, sgu_w, sgu_b, ab_w_out, ab_b_out, rwkv_mu, rwkv_w_rkv, rwkv_w0, rwkv_w_w1, rwkv_w_w2, rwkv_a0, rwkv_a_w1, rwkv_a_w2, rwkv_g_w1, rwkv_g_w2, rwkv_k_k, rwkv_k_a, rwkv_r_k, rwkv_ln_g, rwkv_ln_b, rwkv_w_out, ffn_w_in, ffn_w_out, ln_mix_g, ln_mix_b, ln_ffn_g, ln_ffn_b):
    bsz, seq, d = x.shape
    for layer in range(DEPTH):
        i = layer // 2
        if layer % 2 == 0:
            sgu_b_full = jnp.broadcast_to(sgu_b[i][:, :, None], (SGU_GROUPS, SGU_CHUNK, SGU_GROUP_DIM))
            x = _mixer0(
                x, ab_w_in[i], _row(ab_b_in[i]),
                jnp.broadcast_to(conv_w[i].reshape(CONV_WIDTH, 1, CONV_DIM),
                                 (CONV_WIDTH, CONV_ROW_BLOCK, CONV_DIM)), _row(conv_b[i]),
                _row(conv_norm_g[i]), _row(conv_norm_b[i]), _row(sgu_norm_g[i]), _row(sgu_norm_b[i]),
                sgu_w[i], sgu_b_full, ab_w_out[i], _row(ab_b_out[i]),
                _row(ln_mix_g[layer]), _row(ln_mix_b[layer]))
        else:
            consts = (
                rwkv_mu[i], rwkv_w_rkv[i], _row(rwkv_w0[i]), rwkv_w_w1[i], rwkv_w_w2[i],
                _row(rwkv_a0[i]), rwkv_a_w1[i], rwkv_a_w2[i], rwkv_g_w1[i], rwkv_g_w2[i],
                _row(rwkv_k_k[i]), _row(rwkv_k_a[i]), _row(rwkv_r_k[i]), _row(rwkv_ln_g[i]),
                _row(rwkv_ln_b[i]), rwkv_w_out[i],
                _row(ln_mix_g[layer]), _row(ln_mix_b[layer]))
            x = _rwkv(x, consts)
        x = _ffn(x.reshape(bsz * seq, d), ffn_w_in, ffn_w_out, _row(ln_ffn_g[layer]), _row(ln_ffn_b[layer]),
                 layer, name=f"ffn{layer}").reshape(bsz, seq, d)
    return x
```

```python
import math

import jax
import jax.numpy as jnp
from jax import lax
from jax.experimental import pallas as pl
from jax.experimental.pallas import tpu as pltpu

F32 = jnp.float32
BF16 = jnp.bfloat16

D_MODEL = 1024
DEPTH = 2
ALPHA = (2.0 * DEPTH) ** 0.25
LN_EPS = 1e-5

CONV_DIM = 512
CONV_WIDTH = 31
SGU_DIM = 512
SGU_GROUPS = 4
SGU_GROUP_DIM = SGU_DIM // SGU_GROUPS
SGU_CHUNK = 128

HEAD = 64
GN_EPS = 64e-5
FFN_HIDDEN = 2816

LANES = 128
SUBLANES = 8
MXU_DIM = 256
VMEM_LIMIT_BYTES = 60 * 1024 * 1024

MIX0_ROWS = 512
CONV_HALO = 32
CONV_ROW_BLOCK = 32
FFN_ROWS = 512
FFN_COLS = 256
RWKV_ROWS = 512
RWKV_CHUNK = 64
RWKV_GROUP = 2
RWKV_WEIGHT_SLOTS = (1, 3, 4, 6, 7, 8, 9, 15)
QUAD = MXU_DIM
HEADS_PER_QUAD = QUAD // HEAD


def _dot(a, b):
    return jnp.dot(a, b, preferred_element_type=F32)


def _dot_nt(a, b):
    return lax.dot_general(a, b, (((1,), (1,)), ((), ())), preferred_element_type=F32)


def _layer_norm(x, g, b, eps=LN_EPS):
    mu = jnp.mean(x, axis=-1, keepdims=True)
    xc = x - mu
    var = jnp.mean(xc * xc, axis=-1, keepdims=True)
    return xc * lax.rsqrt(var + eps) * g + b


def _sigmoid(x):
    return 1.0 / (1.0 + jnp.exp(-x))


def _gelu_tanh(x):
    c = math.sqrt(2.0 / math.pi)
    return 0.5 * x * (1.0 + jnp.tanh(c * (x + 0.044715 * (x * x * x))))


def _const_spec(shape):
    nd = len(shape)
    return pl.BlockSpec(shape, lambda *_: (0,) * nd, pipeline_mode=pl.Buffered(1))


def _mixer0_kernel(x_ref, win_ref, bin_ref, cw_ref, cb_ref, cng_ref, cnb_ref, sng_ref, snb_ref,
                   ws_ref, sb_ref, wout_ref, bout_ref, lng_ref, lnb_ref, o_ref,
                   win_b, wout_b, ybuf_ref, yrot_ref, h_ref, mixed_ref):
    rows = MIX0_ROWS
    blk_rows = SGU_CHUNK
    n_blocks = rows // blk_rows
    step = pl.program_id(1)

    @pl.when((pl.program_id(0) == 0) & (step == 0))
    def _():
        win_b[...] = win_ref[...].astype(BF16)
        wout_b[...] = wout_ref[...].astype(BF16)

    @pl.when(step == 0)
    def _():
        ybuf_ref[0:CONV_HALO, :] = jnp.zeros((CONV_HALO, CONV_DIM), F32)

    t_idx = lax.broadcasted_iota(jnp.int32, (SGU_CHUNK, SGU_CHUNK), 0)
    s_idx = lax.broadcasted_iota(jnp.int32, (SGU_CHUNK, SGU_CHUNK), 1)
    w_s = [jnp.where(t_idx >= s_idx, ws_ref[g], 0.0).astype(BF16) for g in range(SGU_GROUPS)]
    first = CONV_HALO - (CONV_WIDTH - 1)
    reach = blk_rows + CONV_HALO - SUBLANES

    def project(b):
        rws = slice(b * blk_rows, (b + 1) * blk_rows)
        h_ref[rws, :] = _dot(x_ref[0, rws, :].astype(BF16), win_b[...]) + bin_ref[...]

    def mix_block(b):
        r0 = b * blk_rows
        rws = slice(r0, r0 + blk_rows)
        ybuf_ref[CONV_HALO + r0:CONV_HALO + r0 + blk_rows, :] = (
            h_ref[rws, 0:CONV_DIM] * _sigmoid(h_ref[rws, CONV_DIM:2 * CONV_DIM]))
        lo = 0 if b == 0 else r0 + reach - blk_rows
        for r in range(1, SUBLANES):
            yrot_ref[r - 1, lo:r0 + reach, :] = ybuf_ref[lo + r:r0 + reach + r, :]
        for sub in range(blk_rows // CONV_ROW_BLOCK):
            base = r0 + sub * CONV_ROW_BLOCK
            acc = jnp.broadcast_to(cb_ref[...], (CONV_ROW_BLOCK, CONV_DIM))
            for j in range(CONV_WIDTH):
                r = (first + j) % SUBLANES
                src = ybuf_ref if r == 0 else yrot_ref.at[r - 1]
                start = base + first + j - r
                acc = acc + cw_ref[j] * src[start:start + CONV_ROW_BLOCK, :]
            ya = _layer_norm(acc, cng_ref[...], cnb_ref[...])
            mixed_ref[base:base + CONV_ROW_BLOCK, 0:CONV_DIM] = (ya * _sigmoid(ya)).astype(BF16)
        u = _gelu_tanh(h_ref[rws, 2 * CONV_DIM:2 * CONV_DIM + SGU_DIM])
        v = _layer_norm(_gelu_tanh(h_ref[rws, 2 * CONV_DIM + SGU_DIM:]), sng_ref[...], snb_ref[...]).astype(BF16)
        for g in range(SGU_GROUPS):
            cols = slice(g * SGU_GROUP_DIM, (g + 1) * SGU_GROUP_DIM)
            sv = _dot(w_s[g], v[:, cols]) + sb_ref[g]
            mixed_ref[rws, CONV_DIM + g * SGU_GROUP_DIM:CONV_DIM + (g + 1) * SGU_GROUP_DIM] = (
                u[:, cols] * sv).astype(BF16)
        mix = _dot(mixed_ref[rws, :], wout_b[...]) + bout_ref[...]
        o_ref[0, rws, :] = _layer_norm(ALPHA * x_ref[0, rws, :] + mix, lng_ref[...], lnb_ref[...])

    project(0)
    for b in range(n_blocks):
        if b + 1 < n_blocks:
            project(b + 1)
        mix_block(b)
    ybuf_ref[0:CONV_HALO, :] = ybuf_ref[rows:rows + CONV_HALO, :]


def _mixer0(x, w_in, b_in, conv_w, conv_b, cn_g, cn_b, sn_g, sn_b, sgu_w, sgu_b_full,
            w_out, b_out, ln_g, ln_b):
    bsz, seq, d = x.shape
    rows = MIX0_ROWS
    x_spec = pl.BlockSpec((1, rows, d), lambda b, i: (b, i, 0))
    consts = (w_in, b_in, conv_w, conv_b, cn_g, cn_b, sn_g, sn_b, sgu_w, sgu_b_full,
              w_out, b_out, ln_g, ln_b)
    return pl.pallas_call(
        _mixer0_kernel,
        grid=(bsz, seq // rows),
        in_specs=[x_spec] + [_const_spec(c.shape) for c in consts],
        out_specs=x_spec,
        out_shape=jax.ShapeDtypeStruct(x.shape, F32),
        scratch_shapes=[
            pltpu.VMEM(w_in.shape, BF16),
            pltpu.VMEM(w_out.shape, BF16),
            pltpu.VMEM((CONV_HALO + rows, CONV_DIM), F32),
            pltpu.VMEM((SUBLANES - 1, CONV_HALO + rows - SUBLANES, CONV_DIM), F32),
            pltpu.VMEM((rows, 2 * CONV_DIM + 2 * SGU_DIM), F32),
            pltpu.VMEM((rows, CONV_DIM + SGU_DIM), BF16),
        ],
        compiler_params=pltpu.CompilerParams(
            dimension_semantics=("arbitrary", "arbitrary"),
            vmem_limit_bytes=VMEM_LIMIT_BYTES),
        name="mixer0",
    )(x, *consts)


def _ffn_kernel(x_ref, win_ref, wout_ref, g_ref, b_ref, o_ref, act_ref):
    x = x_ref[...]
    xb = x.astype(BF16)
    for c in range(FFN_HIDDEN // FFN_COLS):
        lo = c * FFN_COLS
        gate = _dot(xb, win_ref[0, :, lo:lo + FFN_COLS].astype(BF16))
        up = _dot(xb, win_ref[0, :, FFN_HIDDEN + lo:FFN_HIDDEN + lo + FFN_COLS].astype(BF16))
        act_ref[:, lo:lo + FFN_COLS] = (gate * _sigmoid(gate) * up).astype(BF16)
    y = _dot(act_ref[...], wout_ref[0].astype(BF16))
    o_ref[...] = _layer_norm(ALPHA * x + y, g_ref[...], b_ref[...])


def _ffn(x2d, w_in_all, w_out_all, ln_g, ln_b, layer, name):
    n, d = x2d.shape
    rows = FFN_ROWS
    x_spec = pl.BlockSpec((rows, d), lambda i: (i, 0))
    layer_spec = lambda w: pl.BlockSpec((1,) + w.shape[1:], lambda i: (layer, 0, 0),
                                        pipeline_mode=pl.Buffered(1))
    return pl.pallas_call(
        _ffn_kernel,
        grid=(n // rows,),
        in_specs=[x_spec, layer_spec(w_in_all), layer_spec(w_out_all),
                  _const_spec(ln_g.shape), _const_spec(ln_b.shape)],
        out_specs=x_spec,
        out_shape=jax.ShapeDtypeStruct(x2d.shape, F32),
        scratch_shapes=[pltpu.VMEM((rows, FFN_HIDDEN), BF16)],
        compiler_params=pltpu.CompilerParams(
            dimension_semantics=("arbitrary",),
            vmem_limit_bytes=VMEM_LIMIT_BYTES),
        name=name,
    )(x2d, w_in_all, w_out_all, ln_g, ln_b)


def _block_diag(z, same_head):
    tiled = jnp.concatenate([z] * HEADS_PER_QUAD, axis=0)
    return jnp.where(same_head, tiled, jnp.zeros_like(tiled))


def _interleave(*stages):
    live = list(stages)
    while live:
        for stage in list(live):
            try:
                next(stage)
            except StopIteration:
                live.remove(stage)


def _head_sum(z, ones_bd):
    n, width = z.shape
    zb = z.astype(BF16)
    stacked = jnp.concatenate([zb[:, q * QUAD:(q + 1) * QUAD] for q in range(width // QUAD)], axis=0)
    sums = _dot(stacked, ones_bd)
    return jnp.concatenate([sums[q * n:(q + 1) * n] for q in range(width // QUAD)], axis=1)


def _rwkv_kernel(x_ref, mu_ref, wrkv_ref, w0_ref, ww1_ref, ww2_ref, a0_ref,
                 aw1_ref, aw2_ref, gw1_ref, gw2_ref, kk_ref, ka_ref, rk_ref, gng_ref, gnb_ref,
                 wout_ref, lng_ref, lnb_ref, o_ref,
                 wrkv_b, ww1_b, ww2_b, aw1_b, aw2_b, gw1_b, gw2_b, wout_b,
                 xlast_ref, state_ref, r_s, k_s, v_s, kk_s, kka_s, lw_s, g_s, y_s):
    rows = RWKV_ROWS
    chunk = RWKV_CHUNK
    d = D_MODEL
    n_quads = d // QUAD
    step = pl.program_id(1)

    @pl.when((pl.program_id(0) == 0) & (step == 0))
    def _():
        for j in range(3):
            wrkv_b[j] = wrkv_ref[j].astype(BF16)
        for src, dst in ((ww1_ref, ww1_b), (ww2_ref, ww2_b), (aw1_ref, aw1_b), (aw2_ref, aw2_b),
                         (gw1_ref, gw1_b), (gw2_ref, gw2_b), (wout_ref, wout_b)):
            dst[...] = src[...].astype(BF16)

    @pl.when(step == 0)
    def _():
        xlast_ref[...] = jnp.zeros(xlast_ref.shape, F32)
        state_ref[...] = jnp.zeros(state_ref.shape, F32)

    x = x_ref[0]
    row_id = lax.broadcasted_iota(jnp.int32, (rows, d), 0)
    x_prev = jnp.where(row_id == 0, xlast_ref[0:1, :], pltpu.roll(x, 1, axis=0))
    xlast_ref[0:1, :] = x[rows - 1:rows, :]
    xx = x_prev - x

    def mixed(j):
        return (x + xx * mu_ref[j:j + 1, :]).astype(BF16)

    r_i = lax.broadcasted_iota(jnp.int32, (QUAD, QUAD), 0) // HEAD
    c_i = lax.broadcasted_iota(jnp.int32, (QUAD, QUAD), 1) // HEAD
    same_head = r_i == c_i
    ones_bd = jnp.where(same_head, 1.0, 0.0).astype(BF16)

    r = _dot(mixed(0), wrkv_b[0])
    k = _dot(mixed(2), wrkv_b[1])
    v = _dot(mixed(3), wrkv_b[2])
    z_w = w0_ref[...] + _dot(jnp.tanh(_dot(mixed(1), ww1_b[...])).astype(BF16), ww2_b[...])
    lw_s[...] = -math.exp(-0.5) * _sigmoid(z_w)
    a = _sigmoid(a0_ref[...] + _dot(_dot(mixed(4), aw1_b[...]).astype(BF16), aw2_b[...]))
    kk = k * kk_ref[...]
    kk = kk / jnp.maximum(jnp.sqrt(_head_sum(kk * kk, ones_bd)), 1e-12)
    k2 = k * (1.0 + (a - 1.0) * ka_ref[...])
    g_s[...] = _dot(_sigmoid(_dot(mixed(5), gw1_b[...])).astype(BF16), gw2_b[...])
    r_s[...] = r
    k_s[...] = k2
    v_s[...] = v
    kk_s[...] = kk
    kka_s[...] = kk * a

    t_i = lax.broadcasted_iota(jnp.int32, (chunk, chunk), 0)
    s_i = lax.broadcasted_iota(jnp.int32, (chunk, chunk), 1)
    tri_incl = jnp.where(t_i >= s_i, 1.0, 0.0).astype(BF16)
    t_n = lax.broadcasted_iota(jnp.int32, (2 * chunk, QUAD), 0)
    s_n = lax.broadcasted_iota(jnp.int32, (2 * chunk, QUAD), 1) % HEAD
    causal2 = jnp.where(t_n < chunk, t_n, t_n - chunk + 1) > s_n
    eye_nat = jnp.where(lax.broadcasted_iota(jnp.int32, (chunk, QUAD), 0)
                        == lax.broadcasted_iota(jnp.int32, (chunk, QUAD), 1) % HEAD, 1.0, 0.0).astype(F32)

    n_chunks = rows // chunk
    quad = lambda z, q: z[:, q * QUAD:(q + 1) * QUAD]
    bd = lambda z: _block_diag(z, same_head)
    ready = {}

    def prepare(cs):
        items = [(c, q) for c in cs for q in range(n_quads)]
        prep = {}
        for c in cs:
            rws = slice(c * chunk, (c + 1) * chunk)
            lw = lw_s[rws, :]
            lw_hi = lw.astype(BF16)
            lw_lo = (lw - lw_hi.astype(F32)).astype(BF16)
            cum = _dot(tri_incl, lw_hi) + _dot(tri_incl, lw_lo)
            tot = cum[chunk - 1:chunk, :]
            gam_inv = jnp.exp(-cum)
            gam_rest = jnp.exp(tot - cum)
            kk_c = kk_s[rws, :]
            kka_c = kka_s[rws, :]
            k_c = k_s[rws, :]
            prep[c] = dict(
                a_t=(-kk_c * jnp.exp(cum - lw)).astype(BF16),
                r_t=(r_s[rws, :] * jnp.exp(cum)).astype(BF16),
                b_t=(kka_c * gam_inv).astype(BF16),
                k_t=(k_c * gam_inv).astype(BF16),
                b_h=(kka_c * gam_rest).astype(BF16),
                k_h=(k_c * gam_rest).astype(BF16),
                gam_tot=jnp.exp(tot))
        yield
        ar = [jnp.concatenate([quad(prep[c]["a_t"], q), quad(prep[c]["r_t"], q)], axis=0) for c, q in items]
        s_b = [jnp.where(causal2, _dot_nt(ar[i], bd(quad(prep[c]["b_t"], q))), 0.0)
               for i, (c, q) in enumerate(items)]
        akrk = [jnp.where(causal2, _dot_nt(ar[i], bd(quad(prep[c]["k_t"], q))), 0.0).astype(BF16)
                for i, (c, q) in enumerate(items)]
        rb = [z[chunk:].astype(BF16) for z in s_b]
        yield
        r_pow = [z[0:chunk] for z in s_b]
        q_sum = [eye_nat + z for z in r_pow]
        r_pow = [_dot(z.astype(BF16), bd(z.astype(BF16))) for z in r_pow]
        yield
        for _ in range(int(math.log2(chunk)) - 2):
            r_b = [z.astype(BF16) for z in r_pow]
            prod = [_dot(jnp.concatenate([qs.astype(BF16), z], axis=0), bd(z)) for qs, z in zip(q_sum, r_b)]
            q_sum = [qs + p[0:chunk] for qs, p in zip(q_sum, prod)]
            r_pow = [p[chunk:] for p in prod]
            yield
        p_inv = [(qs + _dot(qs.astype(BF16), bd(z.astype(BF16)))).astype(BF16) for qs, z in zip(q_sum, r_pow)]
        for i, (c, q) in enumerate(items):
            ready[c, q] = dict(ar=ar[i], akrk=akrk[i], rb=rb[i], p_inv=p_inv[i],
                               bk=jnp.concatenate([quad(prep[c]["b_h"], q), quad(prep[c]["k_h"], q)], axis=0),
                               gam_tot=quad(prep[c]["gam_tot"], q))

    def advance(c):
        rws = slice(c * chunk, (c + 1) * chunk)
        v_c = v_s[rws, :]
        v_b = v_c.astype(BF16)
        it = [ready.pop((c, q)) for q in range(n_quads)]
        s0 = [state_ref[q] for q in range(n_quads)]
        sv = [_dot_nt(it[q]["ar"], s0[q].astype(BF16)) + _dot(it[q]["akrk"], bd(quad(v_b, q)))
              for q in range(n_quads)]
        yield
        u = [_dot(it[q]["p_inv"], bd(sv[q][0:chunk].astype(BF16))) for q in range(n_quads)]
        yield
        for q in range(n_quads):
            y_s[rws, q * QUAD:(q + 1) * QUAD] = sv[q][chunk:] + _dot(it[q]["rb"], bd(u[q].astype(BF16)))
        for q in range(n_quads):
            uv_t = jnp.concatenate([u[q], quad(v_c, q)], axis=0).T.astype(BF16)
            state_ref[q] = s0[q] * it[q]["gam_tot"] + jnp.where(same_head, _dot(uv_t, it[q]["bk"]), 0.0)
        yield

    def finish(cs):
        rws = slice(cs[0] * chunk, (cs[-1] + 1) * chunk)
        y = y_s[rws, :]
        y_c = y - _head_sum(y, ones_bd) * (1.0 / HEAD)
        yield
        y_var = _head_sum(y_c * y_c, ones_bd) * (1.0 / HEAD)
        bonus = _head_sum(r_s[rws, :] * k_s[rws, :] * rk_ref[...], ones_bd) * v_s[rws, :]
        yield
        y_n = y_c * lax.rsqrt(y_var + GN_EPS) * gng_ref[...] + gnb_ref[...]
        mix = _dot(((y_n + bonus) * g_s[rws, :]).astype(BF16), wout_b[...])
        yield
        o_ref[0, rws, :] = _layer_norm(ALPHA * x_ref[0, rws, :] + mix, lng_ref[...], lnb_ref[...])

    def chain(cs):
        for c in cs:
            yield from advance(c)

    groups = [list(range(g, min(g + RWKV_GROUP, n_chunks))) for g in range(0, n_chunks, RWKV_GROUP)]
    _interleave(prepare(groups[0]))
    for g, cs in enumerate(groups):
        stages = [chain(cs)]
        if g + 1 < len(groups):
            stages.append(prepare(groups[g + 1]))
        if g > 0:
            stages.append(finish(groups[g - 1]))
        _interleave(*stages)
    _interleave(finish(groups[-1]))


def _rwkv(x, consts):
    bsz, seq, d = x.shape
    rows = RWKV_ROWS
    x_spec = pl.BlockSpec((1, rows, d), lambda b, i: (b, i, 0))
    tile = pltpu.VMEM((rows, d), F32)
    return pl.pallas_call(
        _rwkv_kernel,
        grid=(bsz, seq // rows),
        in_specs=[x_spec] + [_const_spec(c.shape) for c in consts],
        out_specs=x_spec,
        out_shape=jax.ShapeDtypeStruct(x.shape, F32),
        scratch_shapes=[pltpu.VMEM(consts[j].shape, BF16) for j in RWKV_WEIGHT_SLOTS] + [
            pltpu.VMEM((SUBLANES, d), F32),
            pltpu.VMEM((d // QUAD, QUAD, QUAD), F32),
            tile, tile, tile, tile, tile, tile, tile, tile,
        ],
        compiler_params=pltpu.CompilerParams(
            dimension_semantics=("arbitrary", "arbitrary"),
            vmem_limit_bytes=VMEM_LIMIT_BYTES),
        name="rwkv7",
    )(x, *consts)


def _row(p):
    return p.reshape(1, -1).astype(F32)


def kernel(x, ab_w_in, ab_b_in, conv_w, conv_b, conv_norm_g, conv_norm_b, sgu_norm_g, sgu_norm_b, sgu_w, sgu_b, ab_w_out, ab_b_out, rwkv_mu, rwkv_w_rkv, rwkv_w0, rwkv_w_w1, rwkv_w_w2, rwkv_a0, rwkv_a_w1, rwkv_a_w2, rwkv_g_w1, rwkv_g_w2, rwkv_k_k, rwkv_k_a, rwkv_r_k, rwkv_ln_g, rwkv_ln_b, rwkv_w_out, ffn_w_in, ffn_w_out, ln_mix_g, ln_mix_b, ln_ffn_g, ln_ffn_b):
    bsz, seq, d = x.shape
    for layer in range(DEPTH):
        i = layer // 2
        if layer % 2 == 0:
            sgu_b_full = jnp.broadcast_to(sgu_b[i][:, :, None], (SGU_GROUPS, SGU_CHUNK, SGU_GROUP_DIM))
            x = _mixer0(
                x, ab_w_in[i], _row(ab_b_in[i]),
                jnp.broadcast_to(conv_w[i].reshape(CONV_WIDTH, 1, CONV_DIM),
                                 (CONV_WIDTH, CONV_ROW_BLOCK, CONV_DIM)), _row(conv_b[i]),
                _row(conv_norm_g[i]), _row(conv_norm_b[i]), _row(sgu_norm_g[i]), _row(sgu_norm_b[i]),
                sgu_w[i], sgu_b_full, ab_w_out[i], _row(ab_b_out[i]),
                _row(ln_mix_g[layer]), _row(ln_mix_b[layer]))
        else:
            consts = (
                rwkv_mu[i], rwkv_w_rkv[i], _row(rwkv_w0[i]), rwkv_w_w1[i], rwkv_w_w2[i],
                _row(rwkv_a0[i]), rwkv_a_w1[i], rwkv_a_w2[i], rwkv_g_w1[i], rwkv_g_w2[i],
                _row(rwkv_k_k[i]), _row(rwkv_k_a[i]), _row(rwkv_r_k[i]), _row(rwkv_ln_g[i]),
                _row(rwkv_ln_b[i]), rwkv_w_out[i],
                _row(ln_mix_g[layer]), _row(ln_mix_b[layer]))
            x = _rwkv(x, consts)
        x = _ffn(x.reshape(bsz * seq, d), ffn_w_in, ffn_w_out, _row(ln_ffn_g[layer]), _row(ln_ffn_b[layer]),
                 layer, name=f"ffn{layer}").reshape(bsz, seq, d)
    return x
```

```python
import math

import jax
import jax.numpy as jnp
from jax import lax
from jax.experimental import pallas as pl
from jax.experimental.pallas import tpu as pltpu

F32 = jnp.float32
BF16 = jnp.bfloat16

D_MODEL = 1024
DEPTH = 2
ALPHA = (2.0 * DEPTH) ** 0.25
LN_EPS = 1e-5

CONV_DIM = 512
CONV_WIDTH = 31
SGU_DIM = 512
SGU_GROUPS = 4
SGU_GROUP_DIM = SGU_DIM // SGU_GROUPS
SGU_CHUNK = 128

HEAD = 64
GN_EPS = 64e-5
FFN_HIDDEN = 2816

LANES = 128
SUBLANES = 8
MXU_DIM = 256
VMEM_LIMIT_BYTES = 60 * 1024 * 1024

MIX0_ROWS = 512
CONV_HALO = 32
CONV_ROW_BLOCK = 32
FFN_ROWS = 512
FFN_COLS = 256
FFN_OUT_PIECES = 2
RWKV_ROWS = 512
RWKV_CHUNK = 64
RWKV_GROUP = 2
RWKV_WEIGHT_SLOTS = (1, 3, 4, 6, 7, 8, 9, 15)
QUAD = MXU_DIM
HEADS_PER_QUAD = QUAD // HEAD


def _dot(a, b):
    return jnp.dot(a, b, preferred_element_type=F32)


def _dot_nt(a, b):
    return lax.dot_general(a, b, (((1,), (1,)), ((), ())), preferred_element_type=F32)


def _layer_norm(x, g, b, eps=LN_EPS):
    mu = jnp.mean(x, axis=-1, keepdims=True)
    xc = x - mu
    var = jnp.mean(xc * xc, axis=-1, keepdims=True)
    return xc * lax.rsqrt(var + eps) * g + b


def _sigmoid(x):
    return 1.0 / (1.0 + jnp.exp(-x))


def _gelu_tanh(x):
    c = math.sqrt(2.0 / math.pi)
    return 0.5 * x * (1.0 + jnp.tanh(c * (x + 0.044715 * (x * x * x))))


def _const_spec(shape):
    nd = len(shape)
    return pl.BlockSpec(shape, lambda *_: (0,) * nd, pipeline_mode=pl.Buffered(1))


def _mixer0_kernel(x_ref, win_ref, bin_ref, cw_ref, cb_ref, cng_ref, cnb_ref, sng_ref, snb_ref,
                   ws_ref, sb_ref, wout_ref, bout_ref, lng_ref, lnb_ref, o_ref,
                   win_b, wout_b, ybuf_ref, yrot_ref, h_ref, mixed_ref):
    rows = MIX0_ROWS
    blk_rows = SGU_CHUNK
    n_blocks = rows // blk_rows
    step = pl.program_id(1)

    @pl.when((pl.program_id(0) == 0) & (step == 0))
    def _():
        win_b[...] = win_ref[...].astype(BF16)
        wout_b[...] = wout_ref[...].astype(BF16)

    @pl.when(step == 0)
    def _():
        ybuf_ref[0:CONV_HALO, :] = jnp.zeros((CONV_HALO, CONV_DIM), F32)

    t_idx = lax.broadcasted_iota(jnp.int32, (SGU_CHUNK, SGU_CHUNK), 0)
    s_idx = lax.broadcasted_iota(jnp.int32, (SGU_CHUNK, SGU_CHUNK), 1)
    w_s = [jnp.where(t_idx >= s_idx, ws_ref[g], 0.0).astype(BF16) for g in range(SGU_GROUPS)]
    first = CONV_HALO - (CONV_WIDTH - 1)
    reach = blk_rows + CONV_HALO - SUBLANES

    def project(b):
        rws = slice(b * blk_rows, (b + 1) * blk_rows)
        h_ref[rws, :] = _dot(x_ref[0, rws, :].astype(BF16), win_b[...]) + bin_ref[...]

    def mix_block(b):
        r0 = b * blk_rows
        rws = slice(r0, r0 + blk_rows)
        ybuf_ref[CONV_HALO + r0:CONV_HALO + r0 + blk_rows, :] = (
            h_ref[rws, 0:CONV_DIM] * _sigmoid(h_ref[rws, CONV_DIM:2 * CONV_DIM]))
        lo = 0 if b == 0 else r0 + reach - blk_rows
        for r in range(1, SUBLANES):
            yrot_ref[r - 1, lo:r0 + reach, :] = ybuf_ref[lo + r:r0 + reach + r, :]
        for sub in range(blk_rows // CONV_ROW_BLOCK):
            base = r0 + sub * CONV_ROW_BLOCK
            acc = jnp.broadcast_to(cb_ref[...], (CONV_ROW_BLOCK, CONV_DIM))
            for j in range(CONV_WIDTH):
                r = (first + j) % SUBLANES
                src = ybuf_ref if r == 0 else yrot_ref.at[r - 1]
                start = base + first + j - r
                acc = acc + cw_ref[j] * src[start:start + CONV_ROW_BLOCK, :]
            ya = _layer_norm(acc, cng_ref[...], cnb_ref[...])
            mixed_ref[base:base + CONV_ROW_BLOCK, 0:CONV_DIM] = (ya * _sigmoid(ya)).astype(BF16)
        u = _gelu_tanh(h_ref[rws, 2 * CONV_DIM:2 * CONV_DIM + SGU_DIM])
        v = _layer_norm(_gelu_tanh(h_ref[rws, 2 * CONV_DIM + SGU_DIM:]), sng_ref[...], snb_ref[...]).astype(BF16)
        for g in range(SGU_GROUPS):
            cols = slice(g * SGU_GROUP_DIM, (g + 1) * SGU_GROUP_DIM)
            sv = _dot(w_s[g], v[:, cols]) + sb_ref[g]
            mixed_ref[rws, CONV_DIM + g * SGU_GROUP_DIM:CONV_DIM + (g + 1) * SGU_GROUP_DIM] = (
                u[:, cols] * sv).astype(BF16)
        mix = _dot(mixed_ref[rws, :], wout_b[...]) + bout_ref[...]
        o_ref[0, rws, :] = _layer_norm(ALPHA * x_ref[0, rws, :] + mix, lng_ref[...], lnb_ref[...])

    project(0)
    for b in range(n_blocks):
        if b + 1 < n_blocks:
            project(b + 1)
        mix_block(b)
    ybuf_ref[0:CONV_HALO, :] = ybuf_ref[rows:rows + CONV_HALO, :]


def _mixer0(x, w_in, b_in, conv_w, conv_b, cn_g, cn_b, sn_g, sn_b, sgu_w, sgu_b_full,
            w_out, b_out, ln_g, ln_b):
    bsz, seq, d = x.shape
    rows = MIX0_ROWS
    x_spec = pl.BlockSpec((1, rows, d), lambda b, i: (b, i, 0))
    consts = (w_in, b_in, conv_w, conv_b, cn_g, cn_b, sn_g, sn_b, sgu_w, sgu_b_full,
              w_out, b_out, ln_g, ln_b)
    return pl.pallas_call(
        _mixer0_kernel,
        grid=(bsz, seq // rows),
        in_specs=[x_spec] + [_const_spec(c.shape) for c in consts],
        out_specs=x_spec,
        out_shape=jax.ShapeDtypeStruct(x.shape, F32),
        scratch_shapes=[
            pltpu.VMEM(w_in.shape, BF16),
            pltpu.VMEM(w_out.shape, BF16),
            pltpu.VMEM((CONV_HALO + rows, CONV_DIM), F32),
            pltpu.VMEM((SUBLANES - 1, CONV_HALO + rows - SUBLANES, CONV_DIM), F32),
            pltpu.VMEM((rows, 2 * CONV_DIM + 2 * SGU_DIM), F32),
            pltpu.VMEM((rows, CONV_DIM + SGU_DIM), BF16),
        ],
        compiler_params=pltpu.CompilerParams(
            dimension_semantics=("arbitrary", "arbitrary"),
            vmem_limit_bytes=VMEM_LIMIT_BYTES),
        name="mixer0",
    )(x, *consts)


def _ffn_kernel(x_ref, win_ref, wout_ref, g_ref, b_ref, o_ref, act_ref):
    x = x_ref[...]
    xb = x.astype(BF16)
    for c in range(FFN_HIDDEN // FFN_COLS):
        lo = c * FFN_COLS
        gate = _dot(xb, win_ref[0, :, lo:lo + FFN_COLS].astype(BF16))
        up = _dot(xb, win_ref[0, :, FFN_HIDDEN + lo:FFN_HIDDEN + lo + FFN_COLS].astype(BF16))
        act_ref[:, lo:lo + FFN_COLS] = (gate * _sigmoid(gate) * up).astype(BF16)
    w_out = wout_ref[0].astype(BF16)
    piece = FFN_ROWS // FFN_OUT_PIECES
    for p in range(FFN_OUT_PIECES):
        rws = slice(p * piece, (p + 1) * piece)
        y = _dot(act_ref[rws, :], w_out)
        o_ref[rws, :] = _layer_norm(ALPHA * x_ref[rws, :] + y, g_ref[...], b_ref[...])


def _ffn(x2d, w_in_all, w_out_all, ln_g, ln_b, layer, name):
    n, d = x2d.shape
    rows = FFN_ROWS
    x_spec = pl.BlockSpec((rows, d), lambda i: (i, 0))
    layer_spec = lambda w: pl.BlockSpec((1,) + w.shape[1:], lambda i: (layer, 0, 0),
                                        pipeline_mode=pl.Buffered(1))
    return pl.pallas_call(
        _ffn_kernel,
        grid=(n // rows,),
        in_specs=[x_spec, layer_spec(w_in_all), layer_spec(w_out_all),
                  _const_spec(ln_g.shape), _const_spec(ln_b.shape)],
        out_specs=x_spec,
        out_shape=jax.ShapeDtypeStruct(x2d.shape, F32),
        scratch_shapes=[pltpu.VMEM((rows, FFN_HIDDEN), BF16)],
        compiler_params=pltpu.CompilerParams(
            dimension_semantics=("arbitrary",),
            vmem_limit_bytes=VMEM_LIMIT_BYTES),
        name=name,
    )(x2d, w_in_all, w_out_all, ln_g, ln_b)


def _block_diag(z, same_head):
    tiled = jnp.concatenate([z] * HEADS_PER_QUAD, axis=0)
    return jnp.where(same_head, tiled, jnp.zeros_like(tiled))


def _interleave(*stages):
    live = list(stages)
    while live:
        for stage in list(live):
            try:
                next(stage)
            except StopIteration:
                live.remove(stage)


def _head_sum(z, ones_bd):
    n, width = z.shape
    zb = z.astype(BF16)
    stacked = jnp.concatenate([zb[:, q * QUAD:(q + 1) * QUAD] for q in range(width // QUAD)], axis=0)
    sums = _dot(stacked, ones_bd)
    return jnp.concatenate([sums[q * n:(q + 1) * n] for q in range(width // QUAD)], axis=1)


def _rwkv_kernel(x_ref, mu_ref, wrkv_ref, w0_ref, ww1_ref, ww2_ref, a0_ref,
                 aw1_ref, aw2_ref, gw1_ref, gw2_ref, kk_ref, ka_ref, rk_ref, gng_ref, gnb_ref,
                 wout_ref, lng_ref, lnb_ref, o_ref,
                 wrkv_b, ww1_b, ww2_b, aw1_b, aw2_b, gw1_b, gw2_b, wout_b,
                 xlast_ref, state_ref, r_s, k_s, v_s, kk_s, kka_s, lw_s, g_s, y_s):
    rows = RWKV_ROWS
    chunk = RWKV_CHUNK
    d = D_MODEL
    n_quads = d // QUAD
    step = pl.program_id(1)

    @pl.when((pl.program_id(0) == 0) & (step == 0))
    def _():
        for j in range(3):
            wrkv_b[j] = wrkv_ref[j].astype(BF16)
        for src, dst in ((ww1_ref, ww1_b), (ww2_ref, ww2_b), (aw1_ref, aw1_b), (aw2_ref, aw2_b),
                         (gw1_ref, gw1_b), (gw2_ref, gw2_b), (wout_ref, wout_b)):
            dst[...] = src[...].astype(BF16)

    @pl.when(step == 0)
    def _():
        xlast_ref[...] = jnp.zeros(xlast_ref.shape, F32)
        state_ref[...] = jnp.zeros(state_ref.shape, F32)

    x = x_ref[0]
    row_id = lax.broadcasted_iota(jnp.int32, (rows, d), 0)
    x_prev = jnp.where(row_id == 0, xlast_ref[0:1, :], pltpu.roll(x, 1, axis=0))
    xlast_ref[0:1, :] = x[rows - 1:rows, :]
    xx = x_prev - x

    def mixed(j):
        return (x + xx * mu_ref[j:j + 1, :]).astype(BF16)

    r_i = lax.broadcasted_iota(jnp.int32, (QUAD, QUAD), 0) // HEAD
    c_i = lax.broadcasted_iota(jnp.int32, (QUAD, QUAD), 1) // HEAD
    same_head = r_i == c_i
    ones_bd = jnp.where(same_head, 1.0, 0.0).astype(BF16)

    r = _dot(mixed(0), wrkv_b[0])
    k = _dot(mixed(2), wrkv_b[1])
    v = _dot(mixed(3), wrkv_b[2])
    z_w = w0_ref[...] + _dot(jnp.tanh(_dot(mixed(1), ww1_b[...])).astype(BF16), ww2_b[...])
    lw_s[...] = -math.exp(-0.5) * _sigmoid(z_w)
    a = _sigmoid(a0_ref[...] + _dot(_dot(mixed(4), aw1_b[...]).astype(BF16), aw2_b[...]))
    kk = k * kk_ref[...]
    kk = kk / jnp.maximum(jnp.sqrt(_head_sum(kk * kk, ones_bd)), 1e-12)
    k2 = k * (1.0 + (a - 1.0) * ka_ref[...])
    g_s[...] = _dot(_sigmoid(_dot(mixed(5), gw1_b[...])).astype(BF16), gw2_b[...])
    r_s[...] = r
    k_s[...] = k2
    v_s[...] = v
    kk_s[...] = kk
    kka_s[...] = kk * a

    t_i = lax.broadcasted_iota(jnp.int32, (chunk, chunk), 0)
    s_i = lax.broadcasted_iota(jnp.int32, (chunk, chunk), 1)
    tri_incl = jnp.where(t_i >= s_i, 1.0, 0.0).astype(BF16)
    t_n = lax.broadcasted_iota(jnp.int32, (2 * chunk, QUAD), 0)
    s_n = lax.broadcasted_iota(jnp.int32, (2 * chunk, QUAD), 1) % HEAD
    causal2 = jnp.where(t_n < chunk, t_n, t_n - chunk + 1) > s_n
    eye_nat = jnp.where(lax.broadcasted_iota(jnp.int32, (chunk, QUAD), 0)
                        == lax.broadcasted_iota(jnp.int32, (chunk, QUAD), 1) % HEAD, 1.0, 0.0).astype(F32)

    n_chunks = rows // chunk
    quad = lambda z, q: z[:, q * QUAD:(q + 1) * QUAD]
    bd = lambda z: _block_diag(z, same_head)
    ready = {}

    def prepare(cs):
        items = [(c, q) for c in cs for q in range(n_quads)]
        prep = {}
        for c in cs:
            rws = slice(c * chunk, (c + 1) * chunk)
            lw = lw_s[rws, :]
            lw_hi = lw.astype(BF16)
            lw_lo = (lw - lw_hi.astype(F32)).astype(BF16)
            cum = _dot(tri_incl, lw_hi) + _dot(tri_incl, lw_lo)
            tot = cum[chunk - 1:chunk, :]
            gam_inv = jnp.exp(-cum)
            gam_rest = jnp.exp(tot - cum)
            kk_c = kk_s[rws, :]
            kka_c = kka_s[rws, :]
            k_c = k_s[rws, :]
            prep[c] = dict(
                a_t=(-kk_c * jnp.exp(cum - lw)).astype(BF16),
                r_t=(r_s[rws, :] * jnp.exp(cum)).astype(BF16),
                b_t=(kka_c * gam_inv).astype(BF16),
                k_t=(k_c * gam_inv).astype(BF16),
                b_h=(kka_c * gam_rest).astype(BF16),
                k_h=(k_c * gam_rest).astype(BF16),
                gam_tot=jnp.exp(tot))
        yield
        ar = [jnp.concatenate([quad(prep[c]["a_t"], q), quad(prep[c]["r_t"], q)], axis=0) for c, q in items]
        s_b = [jnp.where(causal2, _dot_nt(ar[i], bd(quad(prep[c]["b_t"], q))), 0.0)
               for i, (c, q) in enumerate(items)]
        akrk = [jnp.where(causal2, _dot_nt(ar[i], bd(quad(prep[c]["k_t"], q))), 0.0).astype(BF16)
                for i, (c, q) in enumerate(items)]
        rb = [z[chunk:].astype(BF16) for z in s_b]
        yield
        r_pow = [z[0:chunk] for z in s_b]
        q_sum = [eye_nat + z for z in r_pow]
        r_pow = [_dot(z.astype(BF16), bd(z.astype(BF16))) for z in r_pow]
        yield
        for _ in range(int(math.log2(chunk)) - 2):
            r_b = [z.astype(BF16) for z in r_pow]
            prod = [_dot(jnp.concatenate([qs.astype(BF16), z], axis=0), bd(z)) for qs, z in zip(q_sum, r_b)]
            q_sum = [qs + p[0:chunk] for qs, p in zip(q_sum, prod)]
            r_pow = [p[chunk:] for p in prod]
            yield
        p_inv = [(qs + _dot(qs.astype(BF16), bd(z.astype(BF16)))).astype(BF16) for qs, z in zip(q_sum, r_pow)]
        for i, (c, q) in enumerate(items):
            ready[c, q] = dict(ar=ar[i], akrk=akrk[i], rb=rb[i], p_inv=p_inv[i],
                               bk=jnp.concatenate([quad(prep[c]["b_h"], q), quad(prep[c]["k_h"], q)], axis=0),
                               gam_tot=quad(prep[c]["gam_tot"], q))

    def advance(c):
        rws = slice(c * chunk, (c + 1) * chunk)
        v_c = v_s[rws, :]
        v_b = v_c.astype(BF16)
        it = [ready.pop((c, q)) for q in range(n_quads)]
        s0 = [state_ref[q] for q in range(n_quads)]
        sv = [_dot_nt(it[q]["ar"], s0[q].astype(BF16)) + _dot(it[q]["akrk"], bd(quad(v_b, q)))
              for q in range(n_quads)]
        yield
        u = [_dot(it[q]["p_inv"], bd(sv[q][0:chunk].astype(BF16))) for q in range(n_quads)]
        yield
        for q in range(n_quads):
            y_s[rws, q * QUAD:(q + 1) * QUAD] = sv[q][chunk:] + _dot(it[q]["rb"], bd(u[q].astype(BF16)))
        for q in range(n_quads):
            uv_t = jnp.concatenate([u[q], quad(v_c, q)], axis=0).T.astype(BF16)
            state_ref[q] = s0[q] * it[q]["gam_tot"] + jnp.where(same_head, _dot(uv_t, it[q]["bk"]), 0.0)
        yield

    def finish(cs):
        rws = slice(cs[0] * chunk, (cs[-1] + 1) * chunk)
        y = y_s[rws, :]
        y_c = y - _head_sum(y, ones_bd) * (1.0 / HEAD)
        yield
        y_var = _head_sum(y_c * y_c, ones_bd) * (1.0 / HEAD)
        bonus = _head_sum(r_s[rws, :] * k_s[rws, :] * rk_ref[...], ones_bd) * v_s[rws, :]
        yield
        y_n = y_c * lax.rsqrt(y_var + GN_EPS) * gng_ref[...] + gnb_ref[...]
        mix = _dot(((y_n + bonus) * g_s[rws, :]).astype(BF16), wout_b[...])
        yield
        o_ref[0, rws, :] = _layer_norm(ALPHA * x_ref[0, rws, :] + mix, lng_ref[...], lnb_ref[...])

    def chain(cs):
        for c in cs:
            yield from advance(c)

    groups = [list(range(g, min(g + RWKV_GROUP, n_chunks))) for g in range(0, n_chunks, RWKV_GROUP)]
    _interleave(prepare(groups[0]))
    for g, cs in enumerate(groups):
        stages = [chain(cs)]
        if g + 1 < len(groups):
            stages.append(prepare(groups[g + 1]))
        if g > 0:
            stages.append(finish(groups[g - 1]))
        _interleave(*stages)
    _interleave(finish(groups[-1]))


def _rwkv(x, consts):
    bsz, seq, d = x.shape
    rows = RWKV_ROWS
    x_spec = pl.BlockSpec((1, rows, d), lambda b, i: (b, i, 0))
    tile = pltpu.VMEM((rows, d), F32)
    return pl.pallas_call(
        _rwkv_kernel,
        grid=(bsz, seq // rows),
        in_specs=[x_spec] + [_const_spec(c.shape) for c in consts],
        out_specs=x_spec,
        out_shape=jax.ShapeDtypeStruct(x.shape, F32),
        scratch_shapes=[pltpu.VMEM(consts[j].shape, BF16) for j in RWKV_WEIGHT_SLOTS] + [
            pltpu.VMEM((SUBLANES, d), F32),
            pltpu.VMEM((d // QUAD, QUAD, QUAD), F32),
            tile, tile, tile, tile, tile, tile, tile, tile,
        ],
        compiler_params=pltpu.CompilerParams(
            dimension_semantics=("arbitrary", "arbitrary"),
            vmem_limit_bytes=VMEM_LIMIT_BYTES),
        name="rwkv7",
    )(x, *consts)


def _row(p):
    return p.reshape(1, -1).astype(F32)


def kernel(x, ab_w_in, ab_b_in, conv_w, conv_b, conv_norm_g, conv_norm_b, sgu_norm_g, sgu_norm_b, sgu_w, sgu_b, ab_w_out, ab_b_out, rwkv_mu, rwkv_w_rkv, rwkv_w0, rwkv_w_w1, rwkv_w_w2, rwkv_a0, rwkv_a_w1, rwkv_a_w2, rwkv_g_w1, rwkv_g_w2, rwkv_k_k, rwkv_k_a, rwkv_r_k, rwkv_ln_g, rwkv_ln_b, rwkv_w_out, ffn_w_in, ffn_w_out, ln_mix_g, ln_mix_b, ln_ffn_g, ln_ffn_b):
    bsz, seq, d = x.shape
    for layer in range(DEPTH):
        i = layer // 2
        if layer % 2 == 0:
            sgu_b_full = jnp.broadcast_to(sgu_b[i][:, :, None], (SGU_GROUPS, SGU_CHUNK, SGU_GROUP_DIM))
            x = _mixer0(
                x, ab_w_in[i], _row(ab_b_in[i]),
                jnp.broadcast_to(conv_w[i].reshape(CONV_WIDTH, 1, CONV_DIM),
                                 (CONV_WIDTH, CONV_ROW_BLOCK, CONV_DIM)), _row(conv_b[i]),
                _row(conv_norm_g[i]), _row(conv_norm_b[i]), _row(sgu_norm_g[i]), _row(sgu_norm_b[i]),
                sgu_w[i], sgu_b_full, ab_w_out[i], _row(ab_b_out[i]),
                _row(ln_mix_g[layer]), _row(ln_mix_b[layer]))
        else:
            consts = (
                rwkv_mu[i], rwkv_w_rkv[i], _row(rwkv_w0[i]), rwkv_w_w1[i], rwkv_w_w2[i],
                _row(rwkv_a0[i]), rwkv_a_w1[i], rwkv_a_w2[i], rwkv_g_w1[i], rwkv_g_w2[i],
                _row(rwkv_k_k[i]), _row(rwkv_k_a[i]), _row(rwkv_r_k[i]), _row(rwkv_ln_g[i]),
                _row(rwkv_ln_b[i]), rwkv_w_out[i],
                _row(ln_mix_g[layer]), _row(ln_mix_b[layer]))
            x = _rwkv(x, consts)
        x = _ffn(x.reshape(bsz * seq, d), ffn_w_in, ffn_w_out, _row(ln_ffn_g[layer]), _row(ln_ffn_b[layer]),
                 layer, name=f"ffn{layer}").reshape(bsz, seq, d)
    return x
```

```python
import math

import jax
import jax.numpy as jnp
from jax import lax
from jax.experimental import pallas as pl
from jax.experimental.pallas import tpu as pltpu

F32 = jnp.float32
BF16 = jnp.bfloat16

D_MODEL = 1024
DEPTH = 2
ALPHA = (2.0 * DEPTH) ** 0.25
LN_EPS = 1e-5

CONV_DIM = 512
CONV_WIDTH = 31
SGU_DIM = 512
SGU_GROUPS = 4
SGU_GROUP_DIM = SGU_DIM // SGU_GROUPS
SGU_CHUNK = 128

HEAD = 64
GN_EPS = 64e-5
FFN_HIDDEN = 2816

LANES = 128
SUBLANES = 8
MXU_DIM = 256
VMEM_LIMIT_BYTES = 60 * 1024 * 1024

MIX0_ROWS = 512
CONV_HALO = 32
CONV_ROW_BLOCK = 32
FFN_ROWS = 512
FFN_COLS = 256
FFN_OUT_PIECES = 2
RWKV_ROWS = 512
RWKV_CHUNK = 64
RWKV_GROUP = 2
RWKV_WEIGHT_SLOTS = (1, 3, 4, 6, 7, 8, 9, 15)
QUAD = MXU_DIM
HEADS_PER_QUAD = QUAD // HEAD


def _dot(a, b):
    return jnp.dot(a, b, preferred_element_type=F32)


def _dot_nt(a, b):
    return lax.dot_general(a, b, (((1,), (1,)), ((), ())), preferred_element_type=F32)


def _layer_norm(x, g, b, eps=LN_EPS):
    mu = jnp.mean(x, axis=-1, keepdims=True)
    xc = x - mu
    var = jnp.mean(xc * xc, axis=-1, keepdims=True)
    return xc * lax.rsqrt(var + eps) * g + b


def _sigmoid(x):
    return 1.0 / (1.0 + jnp.exp(-x))


def _gelu_tanh(x):
    c = math.sqrt(2.0 / math.pi)
    return 0.5 * x * (1.0 + jnp.tanh(c * (x + 0.044715 * (x * x * x))))


def _const_spec(shape):
    nd = len(shape)
    return pl.BlockSpec(shape, lambda *_: (0,) * nd, pipeline_mode=pl.Buffered(1))


def _mixer0_kernel(x_ref, win_ref, bin_ref, cw_ref, cb_ref, cng_ref, cnb_ref, sng_ref, snb_ref,
                   ws_ref, sb_ref, wout_ref, bout_ref, lng_ref, lnb_ref, o_ref,
                   win_b, wout_b, ybuf_ref, yrot_ref, h_ref, mixed_ref):
    rows = MIX0_ROWS
    blk_rows = SGU_CHUNK
    n_blocks = rows // blk_rows
    step = pl.program_id(1)

    @pl.when((pl.program_id(0) == 0) & (step == 0))
    def _():
        win_b[...] = win_ref[...].astype(BF16)
        wout_b[...] = wout_ref[...].astype(BF16)

    @pl.when(step == 0)
    def _():
        ybuf_ref[0:CONV_HALO, :] = jnp.zeros((CONV_HALO, CONV_DIM), F32)

    t_idx = lax.broadcasted_iota(jnp.int32, (SGU_CHUNK, SGU_CHUNK), 0)
    s_idx = lax.broadcasted_iota(jnp.int32, (SGU_CHUNK, SGU_CHUNK), 1)
    w_s = [jnp.where(t_idx >= s_idx, ws_ref[g], 0.0).astype(BF16) for g in range(SGU_GROUPS)]
    first = CONV_HALO - (CONV_WIDTH - 1)
    reach = blk_rows + CONV_HALO - SUBLANES

    def project(b):
        rws = slice(b * blk_rows, (b + 1) * blk_rows)
        h_ref[rws, :] = _dot(x_ref[0, rws, :].astype(BF16), win_b[...]) + bin_ref[0:1, :]

    def mix_block(b):
        r0 = b * blk_rows
        rws = slice(r0, r0 + blk_rows)
        ybuf_ref[CONV_HALO + r0:CONV_HALO + r0 + blk_rows, :] = (
            h_ref[rws, 0:CONV_DIM] * _sigmoid(h_ref[rws, CONV_DIM:2 * CONV_DIM]))
        lo = 0 if b == 0 else r0 + reach - blk_rows
        for r in range(1, SUBLANES):
            yrot_ref[r - 1, lo:r0 + reach, :] = ybuf_ref[lo + r:r0 + reach + r, :]
        for sub in range(blk_rows // CONV_ROW_BLOCK):
            base = r0 + sub * CONV_ROW_BLOCK
            acc = jnp.broadcast_to(cb_ref[0:1, :], (CONV_ROW_BLOCK, CONV_DIM))
            for j in range(CONV_WIDTH):
                r = (first + j) % SUBLANES
                src = ybuf_ref if r == 0 else yrot_ref.at[r - 1]
                start = base + first + j - r
                acc = acc + cw_ref[j] * src[start:start + CONV_ROW_BLOCK, :]
            ya = _layer_norm(acc, cng_ref[0:1, :], cnb_ref[0:1, :])
            mixed_ref[base:base + CONV_ROW_BLOCK, 0:CONV_DIM] = (ya * _sigmoid(ya)).astype(BF16)
        u = _gelu_tanh(h_ref[rws, 2 * CONV_DIM:2 * CONV_DIM + SGU_DIM])
        v = _layer_norm(_gelu_tanh(h_ref[rws, 2 * CONV_DIM + SGU_DIM:]), sng_ref[0:1, :], snb_ref[0:1, :]).astype(BF16)
        for g in range(SGU_GROUPS):
            cols = slice(g * SGU_GROUP_DIM, (g + 1) * SGU_GROUP_DIM)
            sv = _dot(w_s[g], v[:, cols]) + sb_ref[g]
            mixed_ref[rws, CONV_DIM + g * SGU_GROUP_DIM:CONV_DIM + (g + 1) * SGU_GROUP_DIM] = (
                u[:, cols] * sv).astype(BF16)
        mix = _dot(mixed_ref[rws, :], wout_b[...]) + bout_ref[0:1, :]
        o_ref[0, rws, :] = _layer_norm(ALPHA * x_ref[0, rws, :] + mix, lng_ref[0:1, :], lnb_ref[0:1, :])

    project(0)
    for b in range(n_blocks):
        if b + 1 < n_blocks:
            project(b + 1)
        mix_block(b)
    ybuf_ref[0:CONV_HALO, :] = ybuf_ref[rows:rows + CONV_HALO, :]


def _mixer0(x, w_in, b_in, conv_w, conv_b, cn_g, cn_b, sn_g, sn_b, sgu_w, sgu_b_full,
            w_out, b_out, ln_g, ln_b):
    bsz, seq, d = x.shape
    rows = MIX0_ROWS
    x_spec = pl.BlockSpec((1, rows, d), lambda b, i: (b, i, 0))
    consts = (w_in, b_in, conv_w, conv_b, cn_g, cn_b, sn_g, sn_b, sgu_w, sgu_b_full,
              w_out, b_out, ln_g, ln_b)
    return pl.pallas_call(
        _mixer0_kernel,
        grid=(bsz, seq // rows),
        in_specs=[x_spec] + [_const_spec(c.shape) for c in consts],
        out_specs=x_spec,
        out_shape=jax.ShapeDtypeStruct(x.shape, F32),
        scratch_shapes=[
            pltpu.VMEM(w_in.shape, BF16),
            pltpu.VMEM(w_out.shape, BF16),
            pltpu.VMEM((CONV_HALO + rows, CONV_DIM), F32),
            pltpu.VMEM((SUBLANES - 1, CONV_HALO + rows - SUBLANES, CONV_DIM), F32),
            pltpu.VMEM((rows, 2 * CONV_DIM + 2 * SGU_DIM), F32),
            pltpu.VMEM((rows, CONV_DIM + SGU_DIM), BF16),
        ],
        compiler_params=pltpu.CompilerParams(
            dimension_semantics=("arbitrary", "arbitrary"),
            vmem_limit_bytes=VMEM_LIMIT_BYTES),
        name="mixer0",
    )(x, *consts)


def _ffn_kernel(x_ref, win_ref, wout_ref, g_ref, b_ref, o_ref, act_ref):
    x = x_ref[...]
    xb = x.astype(BF16)
    for c in range(FFN_HIDDEN // FFN_COLS):
        lo = c * FFN_COLS
        gate = _dot(xb, win_ref[0, :, lo:lo + FFN_COLS].astype(BF16))
        up = _dot(xb, win_ref[0, :, FFN_HIDDEN + lo:FFN_HIDDEN + lo + FFN_COLS].astype(BF16))
        act_ref[:, lo:lo + FFN_COLS] = (gate * _sigmoid(gate) * up).astype(BF16)
    w_out = wout_ref[0].astype(BF16)
    piece = FFN_ROWS // FFN_OUT_PIECES
    for p in range(FFN_OUT_PIECES):
        rws = slice(p * piece, (p + 1) * piece)
        y = _dot(act_ref[rws, :], w_out)
        o_ref[rws, :] = _layer_norm(ALPHA * x_ref[rws, :] + y, g_ref[0:1, :], b_ref[0:1, :])


def _ffn(x2d, w_in_all, w_out_all, ln_g, ln_b, layer, name):
    n, d = x2d.shape
    rows = FFN_ROWS
    x_spec = pl.BlockSpec((rows, d), lambda i: (i, 0))
    layer_spec = lambda w: pl.BlockSpec((1,) + w.shape[1:], lambda i: (layer, 0, 0),
                                        pipeline_mode=pl.Buffered(1))
    return pl.pallas_call(
        _ffn_kernel,
        grid=(n // rows,),
        in_specs=[x_spec, layer_spec(w_in_all), layer_spec(w_out_all),
                  _const_spec(ln_g.shape), _const_spec(ln_b.shape)],
        out_specs=x_spec,
        out_shape=jax.ShapeDtypeStruct(x2d.shape, F32),
        scratch_shapes=[pltpu.VMEM((rows, FFN_HIDDEN), BF16)],
        compiler_params=pltpu.CompilerParams(
            dimension_semantics=("arbitrary",),
            vmem_limit_bytes=VMEM_LIMIT_BYTES),
        name=name,
    )(x2d, w_in_all, w_out_all, ln_g, ln_b)


def _block_diag(z, same_head):
    tiled = jnp.concatenate([z] * HEADS_PER_QUAD, axis=0)
    return jnp.where(same_head, tiled, jnp.zeros_like(tiled))


def _interleave(*stages):
    live = list(stages)
    while live:
        for stage in list(live):
            try:
                next(stage)
            except StopIteration:
                live.remove(stage)


def _head_sum(z, ones_bd):
    n, width = z.shape
    zb = z.astype(BF16)
    stacked = jnp.concatenate([zb[:, q * QUAD:(q + 1) * QUAD] for q in range(width // QUAD)], axis=0)
    sums = _dot(stacked, ones_bd)
    return jnp.concatenate([sums[q * n:(q + 1) * n] for q in range(width // QUAD)], axis=1)


def _rwkv_kernel(x_ref, mu_ref, wrkv_ref, w0_ref, ww1_ref, ww2_ref, a0_ref,
                 aw1_ref, aw2_ref, gw1_ref, gw2_ref, kk_ref, ka_ref, rk_ref, gng_ref, gnb_ref,
                 wout_ref, lng_ref, lnb_ref, o_ref,
                 wrkv_b, ww1_b, ww2_b, aw1_b, aw2_b, gw1_b, gw2_b, wout_b,
                 xlast_ref, state_ref, r_s, k_s, v_s, kk_s, kka_s, lw_s, g_s, y_s):
    rows = RWKV_ROWS
    chunk = RWKV_CHUNK
    d = D_MODEL
    n_quads = d // QUAD
    step = pl.program_id(1)

    @pl.when((pl.program_id(0) == 0) & (step == 0))
    def _():
        for j in range(3):
            wrkv_b[j] = wrkv_ref[j].astype(BF16)
        for src, dst in ((ww1_ref, ww1_b), (ww2_ref, ww2_b), (aw1_ref, aw1_b), (aw2_ref, aw2_b),
                         (gw1_ref, gw1_b), (gw2_ref, gw2_b), (wout_ref, wout_b)):
            dst[...] = src[...].astype(BF16)

    @pl.when(step == 0)
    def _():
        xlast_ref[...] = jnp.zeros(xlast_ref.shape, F32)
        state_ref[...] = jnp.zeros(state_ref.shape, F32)

    x = x_ref[0]
    row_id = lax.broadcasted_iota(jnp.int32, (rows, d), 0)
    x_prev = jnp.where(row_id == 0, xlast_ref[0:1, :], pltpu.roll(x, 1, axis=0))
    xlast_ref[0:1, :] = x[rows - 1:rows, :]
    xx = x_prev - x

    def mixed(j):
        return (x + xx * mu_ref[j:j + 1, :]).astype(BF16)

    r_i = lax.broadcasted_iota(jnp.int32, (QUAD, QUAD), 0) // HEAD
    c_i = lax.broadcasted_iota(jnp.int32, (QUAD, QUAD), 1) // HEAD
    same_head = r_i == c_i
    ones_bd = jnp.where(same_head, 1.0, 0.0).astype(BF16)

    r = _dot(mixed(0), wrkv_b[0])
    k = _dot(mixed(2), wrkv_b[1])
    v = _dot(mixed(3), wrkv_b[2])
    z_w = w0_ref[0:1, :] + _dot(jnp.tanh(_dot(mixed(1), ww1_b[...])).astype(BF16), ww2_b[...])
    lw_s[...] = -math.exp(-0.5) * _sigmoid(z_w)
    a = _sigmoid(a0_ref[0:1, :] + _dot(_dot(mixed(4), aw1_b[...]).astype(BF16), aw2_b[...]))
    kk = k * kk_ref[0:1, :]
    kk = kk / jnp.maximum(jnp.sqrt(_head_sum(kk * kk, ones_bd)), 1e-12)
    k2 = k * (1.0 + (a - 1.0) * ka_ref[0:1, :])
    g_s[...] = _dot(_sigmoid(_dot(mixed(5), gw1_b[...])).astype(BF16), gw2_b[...])
    r_s[...] = r
    k_s[...] = k2
    v_s[...] = v
    kk_s[...] = kk
    kka_s[...] = kk * a

    t_i = lax.broadcasted_iota(jnp.int32, (chunk, chunk), 0)
    s_i = lax.broadcasted_iota(jnp.int32, (chunk, chunk), 1)
    tri_incl = jnp.where(t_i >= s_i, 1.0, 0.0).astype(BF16)
    t_n = lax.broadcasted_iota(jnp.int32, (2 * chunk, QUAD), 0)
    s_n = lax.broadcasted_iota(jnp.int32, (2 * chunk, QUAD), 1) % HEAD
    causal2 = jnp.where(t_n < chunk, t_n, t_n - chunk + 1) > s_n
    eye_nat = jnp.where(lax.broadcasted_iota(jnp.int32, (chunk, QUAD), 0)
                        == lax.broadcasted_iota(jnp.int32, (chunk, QUAD), 1) % HEAD, 1.0, 0.0).astype(F32)

    n_chunks = rows // chunk
    quad = lambda z, q: z[:, q * QUAD:(q + 1) * QUAD]
    bd = lambda z: _block_diag(z, same_head)
    ready = {}

    def prepare(cs):
        items = [(c, q) for c in cs for q in range(n_quads)]
        prep = {}
        for c in cs:
            rws = slice(c * chunk, (c + 1) * chunk)
            lw = lw_s[rws, :]
            lw_hi = lw.astype(BF16)
            lw_lo = (lw - lw_hi.astype(F32)).astype(BF16)
            cum = _dot(tri_incl, lw_hi) + _dot(tri_incl, lw_lo)
            tot = cum[chunk - 1:chunk, :]
            gam_inv = jnp.exp(-cum)
            gam_rest = jnp.exp(tot - cum)
            kk_c = kk_s[rws, :]
            kka_c = kka_s[rws, :]
            k_c = k_s[rws, :]
            prep[c] = dict(
                a_t=(-kk_c * jnp.exp(cum - lw)).astype(BF16),
                r_t=(r_s[rws, :] * jnp.exp(cum)).astype(BF16),
                b_t=(kka_c * gam_inv).astype(BF16),
                k_t=(k_c * gam_inv).astype(BF16),
                b_h=(kka_c * gam_rest).astype(BF16),
                k_h=(k_c * gam_rest).astype(BF16),
                gam_tot=jnp.exp(tot))
        yield
        ar = [jnp.concatenate([quad(prep[c]["a_t"], q), quad(prep[c]["r_t"], q)], axis=0) for c, q in items]
        s_b = [jnp.where(causal2, _dot_nt(ar[i], bd(quad(prep[c]["b_t"], q))), 0.0)
               for i, (c, q) in enumerate(items)]
        akrk = [jnp.where(causal2, _dot_nt(ar[i], bd(quad(prep[c]["k_t"], q))), 0.0).astype(BF16)
                for i, (c, q) in enumerate(items)]
        rb = [z[chunk:].astype(BF16) for z in s_b]
        yield
        r_pow = [z[0:chunk] for z in s_b]
        q_sum = [eye_nat + z for z in r_pow]
        r_pow = [_dot(z.astype(BF16), bd(z.astype(BF16))) for z in r_pow]
        yield
        for _ in range(int(math.log2(chunk)) - 2):
            r_b = [z.astype(BF16) for z in r_pow]
            prod = [_dot(jnp.concatenate([qs.astype(BF16), z], axis=0), bd(z)) for qs, z in zip(q_sum, r_b)]
            q_sum = [qs + p[0:chunk] for qs, p in zip(q_sum, prod)]
            r_pow = [p[chunk:] for p in prod]
            yield
        p_inv = [(qs + _dot(qs.astype(BF16), bd(z.astype(BF16)))).astype(BF16) for qs, z in zip(q_sum, r_pow)]
        for i, (c, q) in enumerate(items):
            ready[c, q] = dict(ar=ar[i], akrk=akrk[i], rb=rb[i], p_inv=p_inv[i],
                               bk=jnp.concatenate([quad(prep[c]["b_h"], q), quad(prep[c]["k_h"], q)], axis=0),
                               gam_tot=quad(prep[c]["gam_tot"], q))

    def advance(c):
        rws = slice(c * chunk, (c + 1) * chunk)
        v_c = v_s[rws, :]
        v_b = v_c.astype(BF16)
        it = [ready.pop((c, q)) for q in range(n_quads)]
        s0 = [state_ref[q] for q in range(n_quads)]
        sv = [_dot_nt(it[q]["ar"], s0[q].astype(BF16)) + _dot(it[q]["akrk"], bd(quad(v_b, q)))
              for q in range(n_quads)]
        yield
        u = [_dot(it[q]["p_inv"], bd(sv[q][0:chunk].astype(BF16))) for q in range(n_quads)]
        yield
        for q in range(n_quads):
            y_s[rws, q * QUAD:(q + 1) * QUAD] = sv[q][chunk:] + _dot(it[q]["rb"], bd(u[q].astype(BF16)))
        for q in range(n_quads):
            uv_t = jnp.concatenate([u[q], quad(v_c, q)], axis=0).T.astype(BF16)
            state_ref[q] = s0[q] * it[q]["gam_tot"] + jnp.where(same_head, _dot(uv_t, it[q]["bk"]), 0.0)
        yield

    def finish(cs):
        rws = slice(cs[0] * chunk, (cs[-1] + 1) * chunk)
        y = y_s[rws, :]
        y_c = y - _head_sum(y, ones_bd) * (1.0 / HEAD)
        yield
        y_var = _head_sum(y_c * y_c, ones_bd) * (1.0 / HEAD)
        bonus = _head_sum(r_s[rws, :] * k_s[rws, :] * rk_ref[0:1, :], ones_bd) * v_s[rws, :]
        yield
        y_n = y_c * lax.rsqrt(y_var + GN_EPS) * gng_ref[0:1, :] + gnb_ref[0:1, :]
        mix = _dot(((y_n + bonus) * g_s[rws, :]).astype(BF16), wout_b[...])
        yield
        o_ref[0, rws, :] = _layer_norm(ALPHA * x_ref[0, rws, :] + mix, lng_ref[0:1, :], lnb_ref[0:1, :])

    def chain(cs):
        for c in cs:
            yield from advance(c)

    groups = [list(range(g, min(g + RWKV_GROUP, n_chunks))) for g in range(0, n_chunks, RWKV_GROUP)]
    _interleave(prepare(groups[0]))
    for g, cs in enumerate(groups):
        stages = [chain(cs)]
        if g + 1 < len(groups):
            stages.append(prepare(groups[g + 1]))
        if g > 0:
            stages.append(finish(groups[g - 1]))
        _interleave(*stages)
    _interleave(finish(groups[-1]))


def _rwkv(x, consts):
    bsz, seq, d = x.shape
    rows = RWKV_ROWS
    x_spec = pl.BlockSpec((1, rows, d), lambda b, i: (b, i, 0))
    tile = pltpu.VMEM((rows, d), F32)
    return pl.pallas_call(
        _rwkv_kernel,
        grid=(bsz, seq // rows),
        in_specs=[x_spec] + [_const_spec(c.shape) for c in consts],
        out_specs=x_spec,
        out_shape=jax.ShapeDtypeStruct(x.shape, F32),
        scratch_shapes=[pltpu.VMEM(consts[j].shape, BF16) for j in RWKV_WEIGHT_SLOTS] + [
            pltpu.VMEM((SUBLANES, d), F32),
            pltpu.VMEM((d // QUAD, QUAD, QUAD), F32),
            tile, tile, tile, tile, tile, tile, tile, tile,
        ],
        compiler_params=pltpu.CompilerParams(
            dimension_semantics=("arbitrary", "arbitrary"),
            vmem_limit_bytes=VMEM_LIMIT_BYTES),
        name="rwkv7",
    )(x, *consts)


def _row(p):
    return jnp.broadcast_to(p.reshape(1, -1).astype(F32), (SUBLANES, p.size))


def kernel(x, ab_w_in, ab_b_in, conv_w, conv_b, conv_norm_g, conv_norm_b, sgu_norm_g, sgu_norm_b, sgu_w, sgu_b, ab_w_out, ab_b_out, rwkv_mu, rwkv_w_rkv, rwkv_w0, rwkv_w_w1, rwkv_w_w2, rwkv_a0, rwkv_a_w1, rwkv_a_w2, rwkv_g_w1, rwkv_g_w2, rwkv_k_k, rwkv_k_a, rwkv_r_k, rwkv_ln_g, rwkv_ln_b, rwkv_w_out, ffn_w_in, ffn_w_out, ln_mix_g, ln_mix_b, ln_ffn_g, ln_ffn_b):
    bsz, seq, d = x.shape
    for layer in range(DEPTH):
        i = layer // 2
        if layer % 2 == 0:
            sgu_b_full = jnp.broadcast_to(sgu_b[i][:, :, None], (SGU_GROUPS, SGU_CHUNK, SGU_GROUP_DIM))
            x = _mixer0(
                x, ab_w_in[i], _row(ab_b_in[i]),
                jnp.broadcast_to(conv_w[i].reshape(CONV_WIDTH, 1, CONV_DIM),
                                 (CONV_WIDTH, CONV_ROW_BLOCK, CONV_DIM)), _row(conv_b[i]),
                _row(conv_norm_g[i]), _row(conv_norm_b[i]), _row(sgu_norm_g[i]), _row(sgu_norm_b[i]),
                sgu_w[i], sgu_b_full, ab_w_out[i], _row(ab_b_out[i]),
                _row(ln_mix_g[layer]), _row(ln_mix_b[layer]))
        else:
            consts = (
                jnp.pad(rwkv_mu[i], ((0, SUBLANES - rwkv_mu.shape[1]), (0, 0))), rwkv_w_rkv[i], _row(rwkv_w0[i]), rwkv_w_w1[i], rwkv_w_w2[i],
                _row(rwkv_a0[i]), rwkv_a_w1[i], rwkv_a_w2[i], rwkv_g_w1[i], rwkv_g_w2[i],
                _row(rwkv_k_k[i]), _row(rwkv_k_a[i]), _row(rwkv_r_k[i]), _row(rwkv_ln_g[i]),
                _row(rwkv_ln_b[i]), rwkv_w_out[i],
                _row(ln_mix_g[layer]), _row(ln_mix_b[layer]))
            x = _rwkv(x, consts)
        x = _ffn(x.reshape(bsz * seq, d), ffn_w_in, ffn_w_out, _row(ln_ffn_g[layer]), _row(ln_ffn_b[layer]),
                 layer, name=f"ffn{layer}").reshape(bsz, seq, d)
    return x
```

```python
import math

import jax
import jax.numpy as jnp
from jax import lax
from jax.experimental import pallas as pl
from jax.experimental.pallas import tpu as pltpu

F32 = jnp.float32
BF16 = jnp.bfloat16

D_MODEL = 1024
DEPTH = 2
ALPHA = (2.0 * DEPTH) ** 0.25
LN_EPS = 1e-5

CONV_DIM = 512
CONV_WIDTH = 31
SGU_DIM = 512
SGU_GROUPS = 4
SGU_GROUP_DIM = SGU_DIM // SGU_GROUPS
SGU_CHUNK = 128

HEAD = 64
GN_EPS = 64e-5
FFN_HIDDEN = 2816

LANES = 128
SUBLANES = 8
MXU_DIM = 256
VMEM_LIMIT_BYTES = 60 * 1024 * 1024

MIX0_ROWS = 512
CONV_HALO = 32
CONV_ROW_BLOCK = 32
FFN_ROWS = 512
FFN_COLS = 256
FFN_OUT_PIECES = 2
RWKV_ROWS = 512
RWKV_CHUNK = 64
RWKV_GROUP = 2
RWKV_WEIGHT_SLOTS = (1, 3, 4, 6, 7, 8, 9, 15)
QUAD = MXU_DIM
HEADS_PER_QUAD = QUAD // HEAD


def _dot(a, b):
    return jnp.dot(a, b, preferred_element_type=F32)


def _dot_nt(a, b):
    return lax.dot_general(a, b, (((1,), (1,)), ((), ())), preferred_element_type=F32)


def _layer_norm(x, g, b, eps=LN_EPS):
    mu = jnp.mean(x, axis=-1, keepdims=True)
    xc = x - mu
    var = jnp.mean(xc * xc, axis=-1, keepdims=True)
    return xc * lax.rsqrt(var + eps) * g + b


def _sigmoid(x):
    return 1.0 / (1.0 + jnp.exp(-x))


def _gelu_tanh(x):
    c = math.sqrt(2.0 / math.pi)
    return 0.5 * x * (1.0 + jnp.tanh(c * (x + 0.044715 * (x * x * x))))


def _const_spec(shape):
    nd = len(shape)
    return pl.BlockSpec(shape, lambda *_: (0,) * nd, pipeline_mode=pl.Buffered(1))


def _mixer0_kernel(x_ref, win_ref, bin_ref, cw_ref, cb_ref, cng_ref, cnb_ref, sng_ref, snb_ref,
                   ws_ref, sb_ref, wout_ref, bout_ref, lng_ref, lnb_ref, o_ref,
                   win_b, wout_b, ybuf_ref, yrot_ref, h_ref, mixed_ref):
    rows = MIX0_ROWS
    blk_rows = SGU_CHUNK
    n_blocks = rows // blk_rows
    step = pl.program_id(1)

    @pl.when((pl.program_id(0) == 0) & (step == 0))
    def _():
        win_b[...] = win_ref[...].astype(BF16)
        wout_b[...] = wout_ref[...].astype(BF16)

    @pl.when(step == 0)
    def _():
        ybuf_ref[0:CONV_HALO, :] = jnp.zeros((CONV_HALO, CONV_DIM), F32)

    t_idx = lax.broadcasted_iota(jnp.int32, (SGU_CHUNK, SGU_CHUNK), 0)
    s_idx = lax.broadcasted_iota(jnp.int32, (SGU_CHUNK, SGU_CHUNK), 1)
    w_s = [jnp.where(t_idx >= s_idx, ws_ref[g], 0.0).astype(BF16) for g in range(SGU_GROUPS)]
    first = CONV_HALO - (CONV_WIDTH - 1)
    reach = blk_rows + CONV_HALO - SUBLANES

    def project(b):
        rws = slice(b * blk_rows, (b + 1) * blk_rows)
        h_ref[rws, :] = _dot(x_ref[0, rws, :].astype(BF16), win_b[...]) + bin_ref[...]

    def mix_block(b):
        r0 = b * blk_rows
        rws = slice(r0, r0 + blk_rows)
        ybuf_ref[CONV_HALO + r0:CONV_HALO + r0 + blk_rows, :] = (
            h_ref[rws, 0:CONV_DIM] * _sigmoid(h_ref[rws, CONV_DIM:2 * CONV_DIM]))
        lo = 0 if b == 0 else r0 + reach - blk_rows
        for r in range(1, SUBLANES):
            yrot_ref[r - 1, lo:r0 + reach, :] = ybuf_ref[lo + r:r0 + reach + r, :]
        for sub in range(blk_rows // CONV_ROW_BLOCK):
            base = r0 + sub * CONV_ROW_BLOCK
            acc = jnp.broadcast_to(cb_ref[...], (CONV_ROW_BLOCK, CONV_DIM))
            for j in range(CONV_WIDTH):
                r = (first + j) % SUBLANES
                src = ybuf_ref if r == 0 else yrot_ref.at[r - 1]
                start = base + first + j - r
                acc = acc + cw_ref[j] * src[start:start + CONV_ROW_BLOCK, :]
            ya = _layer_norm(acc, cng_ref[...], cnb_ref[...])
            mixed_ref[base:base + CONV_ROW_BLOCK, 0:CONV_DIM] = (ya * _sigmoid(ya)).astype(BF16)
        u = _gelu_tanh(h_ref[rws, 2 * CONV_DIM:2 * CONV_DIM + SGU_DIM])
        v = _layer_norm(_gelu_tanh(h_ref[rws, 2 * CONV_DIM + SGU_DIM:]), sng_ref[...], snb_ref[...]).astype(BF16)
        for g in range(SGU_GROUPS):
            cols = slice(g * SGU_GROUP_DIM, (g + 1) * SGU_GROUP_DIM)
            sv = _dot(w_s[g], v[:, cols]) + sb_ref[g]
            mixed_ref[rws, CONV_DIM + g * SGU_GROUP_DIM:CONV_DIM + (g + 1) * SGU_GROUP_DIM] = (
                u[:, cols] * sv).astype(BF16)
        mix = _dot(mixed_ref[rws, :], wout_b[...]) + bout_ref[...]
        o_ref[0, rws, :] = _layer_norm(ALPHA * x_ref[0, rws, :] + mix, lng_ref[...], lnb_ref[...])

    project(0)
    for b in range(n_blocks):
        if b + 1 < n_blocks:
            project(b + 1)
        mix_block(b)
    ybuf_ref[0:CONV_HALO, :] = ybuf_ref[rows:rows + CONV_HALO, :]


def _mixer0(x, w_in, b_in, conv_w, conv_b, cn_g, cn_b, sn_g, sn_b, sgu_w, sgu_b_full,
            w_out, b_out, ln_g, ln_b):
    bsz, seq, d = x.shape
    rows = MIX0_ROWS
    x_spec = pl.BlockSpec((1, rows, d), lambda b, i: (b, i, 0))
    consts = (w_in, b_in, conv_w, conv_b, cn_g, cn_b, sn_g, sn_b, sgu_w, sgu_b_full,
              w_out, b_out, ln_g, ln_b)
    return pl.pallas_call(
        _mixer0_kernel,
        grid=(bsz, seq // rows),
        in_specs=[x_spec] + [_const_spec(c.shape) for c in consts],
        out_specs=x_spec,
        out_shape=jax.ShapeDtypeStruct(x.shape, F32),
        scratch_shapes=[
            pltpu.VMEM(w_in.shape, BF16),
            pltpu.VMEM(w_out.shape, BF16),
            pltpu.VMEM((CONV_HALO + rows, CONV_DIM), F32),
            pltpu.VMEM((SUBLANES - 1, CONV_HALO + rows - SUBLANES, CONV_DIM), F32),
            pltpu.VMEM((rows, 2 * CONV_DIM + 2 * SGU_DIM), F32),
            pltpu.VMEM((rows, CONV_DIM + SGU_DIM), BF16),
        ],
        compiler_params=pltpu.CompilerParams(
            dimension_semantics=("arbitrary", "arbitrary"),
            vmem_limit_bytes=VMEM_LIMIT_BYTES),
        name="mixer0",
    )(x, *consts)


def _ffn_kernel(x_ref, win_hbm, wout_hbm, g_ref, b_ref, o_ref, win_v, wout_v, act_ref, in_sem, out_sem,
                *, layer):
    step = pl.program_id(0)
    n_chunks = FFN_HIDDEN // FFN_COLS

    def in_copy(c, half):
        lo = half * FFN_HIDDEN + c * FFN_COLS
        return pltpu.make_async_copy(win_hbm.at[layer, :, pl.ds(lo, FFN_COLS)],
                                     win_v.at[:, pl.ds(lo, FFN_COLS)], in_sem.at[2 * c + half])

    def out_copy(c):
        lo = c * FFN_COLS
        return pltpu.make_async_copy(wout_hbm.at[layer, pl.ds(lo, FFN_COLS), :],
                                     wout_v.at[pl.ds(lo, FFN_COLS), :], out_sem.at[c])

    @pl.when(step == 0)
    def _():
        for c in range(n_chunks):
            in_copy(c, 0).start()
            in_copy(c, 1).start()
        for c in range(n_chunks):
            out_copy(c).start()

    def body(first_step):
        x = x_ref[...]
        xb = x.astype(BF16)
        for c in range(n_chunks):
            lo = c * FFN_COLS
            if first_step:
                in_copy(c, 0).wait()
                in_copy(c, 1).wait()
            gate = _dot(xb, win_v[:, lo:lo + FFN_COLS].astype(BF16))
            up = _dot(xb, win_v[:, FFN_HIDDEN + lo:FFN_HIDDEN + lo + FFN_COLS].astype(BF16))
            act_ref[:, lo:lo + FFN_COLS] = (gate * _sigmoid(gate) * up).astype(BF16)
        if first_step:
            for c in range(n_chunks):
                out_copy(c).wait()
        w_out = wout_v[...].astype(BF16)
        piece = FFN_ROWS // FFN_OUT_PIECES
        for p in range(FFN_OUT_PIECES):
            rws = slice(p * piece, (p + 1) * piece)
            y = _dot(act_ref[rws, :], w_out)
            o_ref[rws, :] = _layer_norm(ALPHA * x_ref[rws, :] + y, g_ref[...], b_ref[...])

    @pl.when(step == 0)
    def _():
        body(True)

    @pl.when(step > 0)
    def _():
        body(False)


def _ffn(x2d, w_in_all, w_out_all, ln_g, ln_b, layer, name):
    n, d = x2d.shape
    rows = FFN_ROWS
    x_spec = pl.BlockSpec((rows, d), lambda i: (i, 0))
    hbm = pl.BlockSpec(memory_space=pl.ANY)
    n_chunks = FFN_HIDDEN // FFN_COLS
    return pl.pallas_call(
        lambda *refs: _ffn_kernel(*refs, layer=layer),
        grid=(n // rows,),
        in_specs=[x_spec, hbm, hbm, _const_spec(ln_g.shape), _const_spec(ln_b.shape)],
        out_specs=x_spec,
        out_shape=jax.ShapeDtypeStruct(x2d.shape, F32),
        scratch_shapes=[
            pltpu.VMEM(w_in_all.shape[1:], F32),
            pltpu.VMEM(w_out_all.shape[1:], F32),
            pltpu.VMEM((rows, FFN_HIDDEN), BF16),
            pltpu.SemaphoreType.DMA((2 * n_chunks,)),
            pltpu.SemaphoreType.DMA((n_chunks,)),
        ],
        compiler_params=pltpu.CompilerParams(
            dimension_semantics=("arbitrary",),
            vmem_limit_bytes=VMEM_LIMIT_BYTES),
        name=name,
    )(x2d, w_in_all, w_out_all, ln_g, ln_b)


def _block_diag(z, same_head):
    tiled = jnp.concatenate([z] * HEADS_PER_QUAD, axis=0)
    return jnp.where(same_head, tiled, jnp.zeros_like(tiled))


def _interleave(*stages):
    live = list(stages)
    while live:
        for stage in list(live):
            try:
                next(stage)
            except StopIteration:
                live.remove(stage)


def _head_sum(z, ones_bd):
    n, width = z.shape
    zb = z.astype(BF16)
    stacked = jnp.concatenate([zb[:, q * QUAD:(q + 1) * QUAD] for q in range(width // QUAD)], axis=0)
    sums = _dot(stacked, ones_bd)
    return jnp.concatenate([sums[q * n:(q + 1) * n] for q in range(width // QUAD)], axis=1)


def _rwkv_kernel(x_ref, mu_ref, wrkv_ref, w0_ref, ww1_ref, ww2_ref, a0_ref,
                 aw1_ref, aw2_ref, gw1_ref, gw2_ref, kk_ref, ka_ref, rk_ref, gng_ref, gnb_ref,
                 wout_ref, lng_ref, lnb_ref, o_ref,
                 wrkv_b, ww1_b, ww2_b, aw1_b, aw2_b, gw1_b, gw2_b, wout_b,
                 xlast_ref, state_ref, r_s, k_s, v_s, kk_s, kka_s, lw_s, g_s, y_s):
    rows = RWKV_ROWS
    chunk = RWKV_CHUNK
    d = D_MODEL
    n_quads = d // QUAD
    step = pl.program_id(1)

    @pl.when((pl.program_id(0) == 0) & (step == 0))
    def _():
        for j in range(3):
            wrkv_b[j] = wrkv_ref[j].astype(BF16)
        for src, dst in ((ww1_ref, ww1_b), (ww2_ref, ww2_b), (aw1_ref, aw1_b), (aw2_ref, aw2_b),
                         (gw1_ref, gw1_b), (gw2_ref, gw2_b), (wout_ref, wout_b)):
            dst[...] = src[...].astype(BF16)

    @pl.when(step == 0)
    def _():
        xlast_ref[...] = jnp.zeros(xlast_ref.shape, F32)
        state_ref[...] = jnp.zeros(state_ref.shape, F32)

    x = x_ref[0]
    row_id = lax.broadcasted_iota(jnp.int32, (rows, d), 0)
    x_prev = jnp.where(row_id == 0, xlast_ref[0:1, :], pltpu.roll(x, 1, axis=0))
    xlast_ref[0:1, :] = x[rows - 1:rows, :]
    xx = x_prev - x

    def mixed(j):
        return (x + xx * mu_ref[j:j + 1, :]).astype(BF16)

    r_i = lax.broadcasted_iota(jnp.int32, (QUAD, QUAD), 0) // HEAD
    c_i = lax.broadcasted_iota(jnp.int32, (QUAD, QUAD), 1) // HEAD
    same_head = r_i == c_i
    ones_bd = jnp.where(same_head, 1.0, 0.0).astype(BF16)

    r = _dot(mixed(0), wrkv_b[0])
    k = _dot(mixed(2), wrkv_b[1])
    v = _dot(mixed(3), wrkv_b[2])
    z_w = w0_ref[...] + _dot(jnp.tanh(_dot(mixed(1), ww1_b[...])).astype(BF16), ww2_b[...])
    lw_s[...] = -math.exp(-0.5) * _sigmoid(z_w)
    a = _sigmoid(a0_ref[...] + _dot(_dot(mixed(4), aw1_b[...]).astype(BF16), aw2_b[...]))
    kk = k * kk_ref[...]
    kk = kk / jnp.maximum(jnp.sqrt(_head_sum(kk * kk, ones_bd)), 1e-12)
    k2 = k * (1.0 + (a - 1.0) * ka_ref[...])
    g_s[...] = _dot(_sigmoid(_dot(mixed(5), gw1_b[...])).astype(BF16), gw2_b[...])
    r_s[...] = r
    k_s[...] = k2
    v_s[...] = v
    kk_s[...] = kk
    kka_s[...] = kk * a

    t_i = lax.broadcasted_iota(jnp.int32, (chunk, chunk), 0)
    s_i = lax.broadcasted_iota(jnp.int32, (chunk, chunk), 1)
    tri_incl = jnp.where(t_i >= s_i, 1.0, 0.0).astype(BF16)
    t_n = lax.broadcasted_iota(jnp.int32, (2 * chunk, QUAD), 0)
    s_n = lax.broadcasted_iota(jnp.int32, (2 * chunk, QUAD), 1) % HEAD
    causal2 = jnp.where(t_n < chunk, t_n, t_n - chunk + 1) > s_n
    eye_nat = jnp.where(lax.broadcasted_iota(jnp.int32, (chunk, QUAD), 0)
                        == lax.broadcasted_iota(jnp.int32, (chunk, QUAD), 1) % HEAD, 1.0, 0.0).astype(F32)

    n_chunks = rows // chunk
    quad = lambda z, q: z[:, q * QUAD:(q + 1) * QUAD]
    bd = lambda z: _block_diag(z, same_head)
    ready = {}

    def prepare(cs):
        items = [(c, q) for c in cs for q in range(n_quads)]
        prep = {}
        for c in cs:
            rws = slice(c * chunk, (c + 1) * chunk)
            lw = lw_s[rws, :]
            lw_hi = lw.astype(BF16)
            lw_lo = (lw - lw_hi.astype(F32)).astype(BF16)
            cum = _dot(tri_incl, lw_hi) + _dot(tri_incl, lw_lo)
            tot = cum[chunk - 1:chunk, :]
            gam_inv = jnp.exp(-cum)
            gam_rest = jnp.exp(tot - cum)
            kk_c = kk_s[rws, :]
            kka_c = kka_s[rws, :]
            k_c = k_s[rws, :]
            prep[c] = dict(
                a_t=(-kk_c * jnp.exp(cum - lw)).astype(BF16),
                r_t=(r_s[rws, :] * jnp.exp(cum)).astype(BF16),
                b_t=(kka_c * gam_inv).astype(BF16),
                k_t=(k_c * gam_inv).astype(BF16),
                b_h=(kka_c * gam_rest).astype(BF16),
                k_h=(k_c * gam_rest).astype(BF16),
                gam_tot=jnp.exp(tot))
        yield
        ar = [jnp.concatenate([quad(prep[c]["a_t"], q), quad(prep[c]["r_t"], q)], axis=0) for c, q in items]
        s_b = [jnp.where(causal2, _dot_nt(ar[i], bd(quad(prep[c]["b_t"], q))), 0.0)
               for i, (c, q) in enumerate(items)]
        akrk = [jnp.where(causal2, _dot_nt(ar[i], bd(quad(prep[c]["k_t"], q))), 0.0).astype(BF16)
                for i, (c, q) in enumerate(items)]
        rb = [z[chunk:].astype(BF16) for z in s_b]
        yield
        r_pow = [z[0:chunk] for z in s_b]
        q_sum = [eye_nat + z for z in r_pow]
        r_pow = [_dot(z.astype(BF16), bd(z.astype(BF16))) for z in r_pow]
        yield
        for _ in range(int(math.log2(chunk)) - 2):
            r_b = [z.astype(BF16) for z in r_pow]
            prod = [_dot(jnp.concatenate([qs.astype(BF16), z], axis=0), bd(z)) for qs, z in zip(q_sum, r_b)]
            q_sum = [qs + p[0:chunk] for qs, p in zip(q_sum, prod)]
            r_pow = [p[chunk:] for p in prod]
            yield
        p_inv = [(qs + _dot(qs.astype(BF16), bd(z.astype(BF16)))).astype(BF16) for qs, z in zip(q_sum, r_pow)]
        for i, (c, q) in enumerate(items):
            ready[c, q] = dict(ar=ar[i], akrk=akrk[i], rb=rb[i], p_inv=p_inv[i],
                               bk=jnp.concatenate([quad(prep[c]["b_h"], q), quad(prep[c]["k_h"], q)], axis=0),
                               gam_tot=quad(prep[c]["gam_tot"], q))

    def advance(c):
        rws = slice(c * chunk, (c + 1) * chunk)
        v_c = v_s[rws, :]
        v_b = v_c.astype(BF16)
        it = [ready.pop((c, q)) for q in range(n_quads)]
        s0 = [state_ref[q] for q in range(n_quads)]
        sv = [_dot_nt(it[q]["ar"], s0[q].astype(BF16)) + _dot(it[q]["akrk"], bd(quad(v_b, q)))
              for q in range(n_quads)]
        yield
        u = [_dot(it[q]["p_inv"], bd(sv[q][0:chunk].astype(BF16))) for q in range(n_quads)]
        yield
        for q in range(n_quads):
            y_s[rws, q * QUAD:(q + 1) * QUAD] = sv[q][chunk:] + _dot(it[q]["rb"], bd(u[q].astype(BF16)))
        for q in range(n_quads):
            uv_t = jnp.concatenate([u[q], quad(v_c, q)], axis=0).T.astype(BF16)
            state_ref[q] = s0[q] * it[q]["gam_tot"] + jnp.where(same_head, _dot(uv_t, it[q]["bk"]), 0.0)
        yield

    def finish(cs):
        rws = slice(cs[0] * chunk, (cs[-1] + 1) * chunk)
        y = y_s[rws, :]
        y_c = y - _head_sum(y, ones_bd) * (1.0 / HEAD)
        yield
        y_var = _head_sum(y_c * y_c, ones_bd) * (1.0 / HEAD)
        bonus = _head_sum(r_s[rws, :] * k_s[rws, :] * rk_ref[...], ones_bd) * v_s[rws, :]
        yield
        y_n = y_c * lax.rsqrt(y_var + GN_EPS) * gng_ref[...] + gnb_ref[...]
        mix = _dot(((y_n + bonus) * g_s[rws, :]).astype(BF16), wout_b[...])
        yield
        o_ref[0, rws, :] = _layer_norm(ALPHA * x_ref[0, rws, :] + mix, lng_ref[...], lnb_ref[...])

    def chain(cs):
        for c in cs:
            yield from advance(c)

    groups = [list(range(g, min(g + RWKV_GROUP, n_chunks))) for g in range(0, n_chunks, RWKV_GROUP)]
    _interleave(prepare(groups[0]))
    for g, cs in enumerate(groups):
        stages = [chain(cs)]
        if g + 1 < len(groups):
            stages.append(prepare(groups[g + 1]))
        if g > 0:
            stages.append(finish(groups[g - 1]))
        _interleave(*stages)
    _interleave(finish(groups[-1]))


def _rwkv(x, consts):
    bsz, seq, d = x.shape
    rows = RWKV_ROWS
    x_spec = pl.BlockSpec((1, rows, d), lambda b, i: (b, i, 0))
    tile = pltpu.VMEM((rows, d), F32)
    return pl.pallas_call(
        _rwkv_kernel,
        grid=(bsz, seq // rows),
        in_specs=[x_spec] + [_const_spec(c.shape) for c in consts],
        out_specs=x_spec,
        out_shape=jax.ShapeDtypeStruct(x.shape, F32),
        scratch_shapes=[pltpu.VMEM(consts[j].shape, BF16) for j in RWKV_WEIGHT_SLOTS] + [
            pltpu.VMEM((SUBLANES, d), F32),
            pltpu.VMEM((d // QUAD, QUAD, QUAD), F32),
            tile, tile, tile, tile, tile, tile, tile, tile,
        ],
        compiler_params=pltpu.CompilerParams(
            dimension_semantics=("arbitrary", "arbitrary"),
            vmem_limit_bytes=VMEM_LIMIT_BYTES),
        name="rwkv7",
    )(x, *consts)


def _row(p):
    return p.reshape(1, -1).astype(F32)


def kernel(x, ab_w_in, ab_b_in, conv_w, conv_b, conv_norm_g, conv_norm_b, sgu_norm_g, sgu_norm_b, sgu_w, sgu_b, ab_w_out, ab_b_out, rwkv_mu, rwkv_w_rkv, rwkv_w0, rwkv_w_w1, rwkv_w_w2, rwkv_a0, rwkv_a_w1, rwkv_a_w2, rwkv_g_w1, rwkv_g_w2, rwkv_k_k, rwkv_k_a, rwkv_r_k, rwkv_ln_g, rwkv_ln_b, rwkv_w_out, ffn_w_in, ffn_w_out, ln_mix_g, ln_mix_b, ln_ffn_g, ln_ffn_b):
    bsz, seq, d = x.shape
    for layer in range(DEPTH):
        i = layer // 2
        if layer % 2 == 0:
            sgu_b_full = jnp.broadcast_to(sgu_b[i][:, :, None], (SGU_GROUPS, SGU_CHUNK, SGU_GROUP_DIM))
            x = _mixer0(
                x, ab_w_in[i], _row(ab_b_in[i]),
                jnp.broadcast_to(conv_w[i].reshape(CONV_WIDTH, 1, CONV_DIM),
                                 (CONV_WIDTH, CONV_ROW_BLOCK, CONV_DIM)), _row(conv_b[i]),
                _row(conv_norm_g[i]), _row(conv_norm_b[i]), _row(sgu_norm_g[i]), _row(sgu_norm_b[i]),
                sgu_w[i], sgu_b_full, ab_w_out[i], _row(ab_b_out[i]),
                _row(ln_mix_g[layer]), _row(ln_mix_b[layer]))
        else:
            consts = (
                rwkv_mu[i], rwkv_w_rkv[i], _row(rwkv_w0[i]), rwkv_w_w1[i], rwkv_w_w2[i],
                _row(rwkv_a0[i]), rwkv_a_w1[i], rwkv_a_w2[i], rwkv_g_w1[i], rwkv_g_w2[i],
                _row(rwkv_k_k[i]), _row(rwkv_k_a[i]), _row(rwkv_r_k[i]), _row(rwkv_ln_g[i]),
                _row(rwkv_ln_b[i]), rwkv_w_out[i],
                _row(ln_mix_g[layer]), _row(ln_mix_b[layer]))
            x = _rwkv(x, consts)
        x = _ffn(x.reshape(bsz * seq, d), ffn_w_in, ffn_w_out, _row(ln_ffn_g[layer]), _row(ln_ffn_b[layer]),
                 layer, name=f"ffn{layer}").reshape(bsz, seq, d)
    return x
```

```python
import math

import jax
import jax.numpy as jnp
from jax import lax
from jax.experimental import pallas as pl
from jax.experimental.pallas import tpu as pltpu

F32 = jnp.float32
BF16 = jnp.bfloat16

D_MODEL = 1024
DEPTH = 2
ALPHA = (2.0 * DEPTH) ** 0.25
LN_EPS = 1e-5

CONV_DIM = 512
CONV_WIDTH = 31
SGU_DIM = 512
SGU_GROUPS = 4
SGU_GROUP_DIM = SGU_DIM // SGU_GROUPS
SGU_CHUNK = 128

HEAD = 64
GN_EPS = 64e-5
FFN_HIDDEN = 2816

LANES = 128
SUBLANES = 8
MXU_DIM = 256
VMEM_LIMIT_BYTES = 60 * 1024 * 1024

MIX0_ROWS = 512
CONV_HALO = 32
CONV_ROW_BLOCK = 32
FFN_ROWS = 512
FFN_COLS = 256
FFN_OUT_PIECES = 2
RWKV_ROWS = 256
RWKV_SEQS = 2
RWKV_CHUNK = 64
RWKV_GROUP = 2
RWKV_WEIGHT_SLOTS = (1, 3, 4, 6, 7, 8, 9, 15)
QUAD = MXU_DIM
HEADS_PER_QUAD = QUAD // HEAD


def _dot(a, b):
    return jnp.dot(a, b, preferred_element_type=F32)


def _dot_nt(a, b):
    return lax.dot_general(a, b, (((1,), (1,)), ((), ())), preferred_element_type=F32)


def _layer_norm(x, g, b, eps=LN_EPS):
    mu = jnp.mean(x, axis=-1, keepdims=True)
    xc = x - mu
    var = jnp.mean(xc * xc, axis=-1, keepdims=True)
    return xc * lax.rsqrt(var + eps) * g + b


def _sigmoid(x):
    return 1.0 / (1.0 + jnp.exp(-x))


def _gelu_tanh(x):
    c = math.sqrt(2.0 / math.pi)
    return 0.5 * x * (1.0 + jnp.tanh(c * (x + 0.044715 * (x * x * x))))


def _const_spec(shape):
    nd = len(shape)
    return pl.BlockSpec(shape, lambda *_: (0,) * nd, pipeline_mode=pl.Buffered(1))


def _mixer0_kernel(x_ref, win_ref, bin_ref, cw_ref, cb_ref, cng_ref, cnb_ref, sng_ref, snb_ref,
                   ws_ref, sb_ref, wout_ref, bout_ref, lng_ref, lnb_ref, o_ref,
                   win_b, wout_b, ybuf_ref, yrot_ref, h_ref, mixed_ref):
    rows = MIX0_ROWS
    blk_rows = SGU_CHUNK
    n_blocks = rows // blk_rows
    step = pl.program_id(1)

    @pl.when((pl.program_id(0) == 0) & (step == 0))
    def _():
        win_b[...] = win_ref[...].astype(BF16)
        wout_b[...] = wout_ref[...].astype(BF16)

    @pl.when(step == 0)
    def _():
        ybuf_ref[0:CONV_HALO, :] = jnp.zeros((CONV_HALO, CONV_DIM), F32)

    t_idx = lax.broadcasted_iota(jnp.int32, (SGU_CHUNK, SGU_CHUNK), 0)
    s_idx = lax.broadcasted_iota(jnp.int32, (SGU_CHUNK, SGU_CHUNK), 1)
    w_s = [jnp.where(t_idx >= s_idx, ws_ref[g], 0.0).astype(BF16) for g in range(SGU_GROUPS)]
    first = CONV_HALO - (CONV_WIDTH - 1)
    reach = blk_rows + CONV_HALO - SUBLANES

    def project(b):
        rws = slice(b * blk_rows, (b + 1) * blk_rows)
        h_ref[rws, :] = _dot(x_ref[0, rws, :].astype(BF16), win_b[...]) + bin_ref[...]

    def mix_block(b):
        r0 = b * blk_rows
        rws = slice(r0, r0 + blk_rows)
        ybuf_ref[CONV_HALO + r0:CONV_HALO + r0 + blk_rows, :] = (
            h_ref[rws, 0:CONV_DIM] * _sigmoid(h_ref[rws, CONV_DIM:2 * CONV_DIM]))
        lo = 0 if b == 0 else r0 + reach - blk_rows
        for r in range(1, SUBLANES):
            yrot_ref[r - 1, lo:r0 + reach, :] = ybuf_ref[lo + r:r0 + reach + r, :]
        for sub in range(blk_rows // CONV_ROW_BLOCK):
            base = r0 + sub * CONV_ROW_BLOCK
            acc = jnp.broadcast_to(cb_ref[...], (CONV_ROW_BLOCK, CONV_DIM))
            for j in range(CONV_WIDTH):
                r = (first + j) % SUBLANES
                src = ybuf_ref if r == 0 else yrot_ref.at[r - 1]
                start = base + first + j - r
                acc = acc + cw_ref[j] * src[start:start + CONV_ROW_BLOCK, :]
            ya = _layer_norm(acc, cng_ref[...], cnb_ref[...])
            mixed_ref[base:base + CONV_ROW_BLOCK, 0:CONV_DIM] = (ya * _sigmoid(ya)).astype(BF16)
        u = _gelu_tanh(h_ref[rws, 2 * CONV_DIM:2 * CONV_DIM + SGU_DIM])
        v = _layer_norm(_gelu_tanh(h_ref[rws, 2 * CONV_DIM + SGU_DIM:]), sng_ref[...], snb_ref[...]).astype(BF16)
        for g in range(SGU_GROUPS):
            cols = slice(g * SGU_GROUP_DIM, (g + 1) * SGU_GROUP_DIM)
            sv = _dot(w_s[g], v[:, cols]) + sb_ref[g]
            mixed_ref[rws, CONV_DIM + g * SGU_GROUP_DIM:CONV_DIM + (g + 1) * SGU_GROUP_DIM] = (
                u[:, cols] * sv).astype(BF16)
        mix = _dot(mixed_ref[rws, :], wout_b[...]) + bout_ref[...]
        o_ref[0, rws, :] = _layer_norm(ALPHA * x_ref[0, rws, :] + mix, lng_ref[...], lnb_ref[...])

    project(0)
    for b in range(n_blocks):
        if b + 1 < n_blocks:
            project(b + 1)
        mix_block(b)
    ybuf_ref[0:CONV_HALO, :] = ybuf_ref[rows:rows + CONV_HALO, :]


def _mixer0(x, w_in, b_in, conv_w, conv_b, cn_g, cn_b, sn_g, sn_b, sgu_w, sgu_b_full,
            w_out, b_out, ln_g, ln_b):
    bsz, seq, d = x.shape
    rows = MIX0_ROWS
    x_spec = pl.BlockSpec((1, rows, d), lambda b, i: (b, i, 0))
    consts = (w_in, b_in, conv_w, conv_b, cn_g, cn_b, sn_g, sn_b, sgu_w, sgu_b_full,
              w_out, b_out, ln_g, ln_b)
    return pl.pallas_call(
        _mixer0_kernel,
        grid=(bsz, seq // rows),
        in_specs=[x_spec] + [_const_spec(c.shape) for c in consts],
        out_specs=x_spec,
        out_shape=jax.ShapeDtypeStruct(x.shape, F32),
        scratch_shapes=[
            pltpu.VMEM(w_in.shape, BF16),
            pltpu.VMEM(w_out.shape, BF16),
            pltpu.VMEM((CONV_HALO + rows, CONV_DIM), F32),
            pltpu.VMEM((SUBLANES - 1, CONV_HALO + rows - SUBLANES, CONV_DIM), F32),
            pltpu.VMEM((rows, 2 * CONV_DIM + 2 * SGU_DIM), F32),
            pltpu.VMEM((rows, CONV_DIM + SGU_DIM), BF16),
        ],
        compiler_params=pltpu.CompilerParams(
            dimension_semantics=("arbitrary", "arbitrary"),
            vmem_limit_bytes=VMEM_LIMIT_BYTES),
        name="mixer0",
    )(x, *consts)


def _ffn_kernel(x_ref, win_ref, wout_ref, g_ref, b_ref, o_ref, act_ref):
    x = x_ref[...]
    xb = x.astype(BF16)
    for c in range(FFN_HIDDEN // FFN_COLS):
        lo = c * FFN_COLS
        gate = _dot(xb, win_ref[0, :, lo:lo + FFN_COLS].astype(BF16))
        up = _dot(xb, win_ref[0, :, FFN_HIDDEN + lo:FFN_HIDDEN + lo + FFN_COLS].astype(BF16))
        act_ref[:, lo:lo + FFN_COLS] = (gate * _sigmoid(gate) * up).astype(BF16)
    w_out = wout_ref[0].astype(BF16)
    piece = FFN_ROWS // FFN_OUT_PIECES
    for p in range(FFN_OUT_PIECES):
        rws = slice(p * piece, (p + 1) * piece)
        y = _dot(act_ref[rws, :], w_out)
        o_ref[rws, :] = _layer_norm(ALPHA * x_ref[rws, :] + y, g_ref[...], b_ref[...])


def _ffn(x2d, w_in_all, w_out_all, ln_g, ln_b, layer, name):
    n, d = x2d.shape
    rows = FFN_ROWS
    x_spec = pl.BlockSpec((rows, d), lambda i: (i, 0))
    layer_spec = lambda w: pl.BlockSpec((1,) + w.shape[1:], lambda i: (layer, 0, 0),
                                        pipeline_mode=pl.Buffered(1))
    return pl.pallas_call(
        _ffn_kernel,
        grid=(n // rows,),
        in_specs=[x_spec, layer_spec(w_in_all), layer_spec(w_out_all),
                  _const_spec(ln_g.shape), _const_spec(ln_b.shape)],
        out_specs=x_spec,
        out_shape=jax.ShapeDtypeStruct(x2d.shape, F32),
        scratch_shapes=[pltpu.VMEM((rows, FFN_HIDDEN), BF16)],
        compiler_params=pltpu.CompilerParams(
            dimension_semantics=("arbitrary",),
            vmem_limit_bytes=VMEM_LIMIT_BYTES),
        name=name,
    )(x2d, w_in_all, w_out_all, ln_g, ln_b)


def _block_diag(z, same_head):
    tiled = jnp.concatenate([z] * HEADS_PER_QUAD, axis=0)
    return jnp.where(same_head, tiled, jnp.zeros_like(tiled))


def _interleave(*stages):
    live = list(stages)
    while live:
        for stage in list(live):
            try:
                next(stage)
            except StopIteration:
                live.remove(stage)


def _head_sum(z, ones_bd):
    n, width = z.shape
    zb = z.astype(BF16)
    stacked = jnp.concatenate([zb[:, q * QUAD:(q + 1) * QUAD] for q in range(width // QUAD)], axis=0)
    sums = _dot(stacked, ones_bd)
    return jnp.concatenate([sums[q * n:(q + 1) * n] for q in range(width // QUAD)], axis=1)


def _rwkv_kernel(x_ref, mu_ref, wrkv_ref, w0_ref, ww1_ref, ww2_ref, a0_ref,
                 aw1_ref, aw2_ref, gw1_ref, gw2_ref, kk_ref, ka_ref, rk_ref, gng_ref, gnb_ref,
                 wout_ref, lng_ref, lnb_ref, o_ref,
                 wrkv_b, ww1_b, ww2_b, aw1_b, aw2_b, gw1_b, gw2_b, wout_b,
                 xlast_ref, state_ref, r_s, k_s, v_s, kk_s, kka_s, lw_s, g_s, y_s):
    seq_rows = RWKV_ROWS
    rows = RWKV_SEQS * seq_rows
    chunk = RWKV_CHUNK
    d = D_MODEL
    n_quads = d // QUAD
    step = pl.program_id(1)

    @pl.when((pl.program_id(0) == 0) & (step == 0))
    def _():
        for j in range(3):
            wrkv_b[j] = wrkv_ref[j].astype(BF16)
        for src, dst in ((ww1_ref, ww1_b), (ww2_ref, ww2_b), (aw1_ref, aw1_b), (aw2_ref, aw2_b),
                         (gw1_ref, gw1_b), (gw2_ref, gw2_b), (wout_ref, wout_b)):
            dst[...] = src[...].astype(BF16)

    @pl.when(step == 0)
    def _():
        xlast_ref[...] = jnp.zeros(xlast_ref.shape, F32)
        state_ref[...] = jnp.zeros(state_ref.shape, F32)

    x = jnp.concatenate([x_ref[s] for s in range(RWKV_SEQS)], axis=0)
    row_id = lax.broadcasted_iota(jnp.int32, (rows, d), 0)
    x_prev = pltpu.roll(x, 1, axis=0)
    for s in range(RWKV_SEQS):
        x_prev = jnp.where(row_id == s * seq_rows, xlast_ref[s:s + 1, :], x_prev)
        xlast_ref[s:s + 1, :] = x[(s + 1) * seq_rows - 1:(s + 1) * seq_rows, :]
    xx = x_prev - x

    def mixed(j):
        return (x + xx * mu_ref[j:j + 1, :]).astype(BF16)

    r_i = lax.broadcasted_iota(jnp.int32, (QUAD, QUAD), 0) // HEAD
    c_i = lax.broadcasted_iota(jnp.int32, (QUAD, QUAD), 1) // HEAD
    same_head = r_i == c_i
    ones_bd = jnp.where(same_head, 1.0, 0.0).astype(BF16)

    r = _dot(mixed(0), wrkv_b[0])
    k = _dot(mixed(2), wrkv_b[1])
    v = _dot(mixed(3), wrkv_b[2])
    z_w = w0_ref[...] + _dot(jnp.tanh(_dot(mixed(1), ww1_b[...])).astype(BF16), ww2_b[...])
    lw_s[...] = -math.exp(-0.5) * _sigmoid(z_w)
    a = _sigmoid(a0_ref[...] + _dot(_dot(mixed(4), aw1_b[...]).astype(BF16), aw2_b[...]))
    kk = k * kk_ref[...]
    kk = kk / jnp.maximum(jnp.sqrt(_head_sum(kk * kk, ones_bd)), 1e-12)
    k2 = k * (1.0 + (a - 1.0) * ka_ref[...])
    g_s[...] = _dot(_sigmoid(_dot(mixed(5), gw1_b[...])).astype(BF16), gw2_b[...])
    r_s[...] = r
    k_s[...] = k2
    v_s[...] = v
    kk_s[...] = kk
    kka_s[...] = kk * a

    t_i = lax.broadcasted_iota(jnp.int32, (chunk, chunk), 0)
    s_i = lax.broadcasted_iota(jnp.int32, (chunk, chunk), 1)
    tri_incl = jnp.where(t_i >= s_i, 1.0, 0.0).astype(BF16)
    t_n = lax.broadcasted_iota(jnp.int32, (2 * chunk, QUAD), 0)
    s_n = lax.broadcasted_iota(jnp.int32, (2 * chunk, QUAD), 1) % HEAD
    causal2 = jnp.where(t_n < chunk, t_n, t_n - chunk + 1) > s_n
    eye_nat = jnp.where(lax.broadcasted_iota(jnp.int32, (chunk, QUAD), 0)
                        == lax.broadcasted_iota(jnp.int32, (chunk, QUAD), 1) % HEAD, 1.0, 0.0).astype(F32)

    n_chunks = rows // chunk
    quad = lambda z, q: z[:, q * QUAD:(q + 1) * QUAD]
    bd = lambda z: _block_diag(z, same_head)
    ready = {}

    def prepare(cs):
        items = [(c, q) for c in cs for q in range(n_quads)]
        prep = {}
        for c in cs:
            rws = slice(c * chunk, (c + 1) * chunk)
            lw = lw_s[rws, :]
            lw_hi = lw.astype(BF16)
            lw_lo = (lw - lw_hi.astype(F32)).astype(BF16)
            cum = _dot(tri_incl, lw_hi) + _dot(tri_incl, lw_lo)
            tot = cum[chunk - 1:chunk, :]
            gam_inv = jnp.exp(-cum)
            gam_rest = jnp.exp(tot - cum)
            kk_c = kk_s[rws, :]
            kka_c = kka_s[rws, :]
            k_c = k_s[rws, :]
            prep[c] = dict(
                a_t=(-kk_c * jnp.exp(cum - lw)).astype(BF16),
                r_t=(r_s[rws, :] * jnp.exp(cum)).astype(BF16),
                b_t=(kka_c * gam_inv).astype(BF16),
                k_t=(k_c * gam_inv).astype(BF16),
                b_h=(kka_c * gam_rest).astype(BF16),
                k_h=(k_c * gam_rest).astype(BF16),
                gam_tot=jnp.exp(tot))
        yield
        ar = [jnp.concatenate([quad(prep[c]["a_t"], q), quad(prep[c]["r_t"], q)], axis=0) for c, q in items]
        s_b = [jnp.where(causal2, _dot_nt(ar[i], bd(quad(prep[c]["b_t"], q))), 0.0)
               for i, (c, q) in enumerate(items)]
        akrk = [jnp.where(causal2, _dot_nt(ar[i], bd(quad(prep[c]["k_t"], q))), 0.0).astype(BF16)
                for i, (c, q) in enumerate(items)]
        rb = [z[chunk:].astype(BF16) for z in s_b]
        yield
        r_pow = [z[0:chunk] for z in s_b]
        q_sum = [eye_nat + z for z in r_pow]
        r_pow = [_dot(z.astype(BF16), bd(z.astype(BF16))) for z in r_pow]
        yield
        for _ in range(int(math.log2(chunk)) - 2):
            r_b = [z.astype(BF16) for z in r_pow]
            prod = [_dot(jnp.concatenate([qs.astype(BF16), z], axis=0), bd(z)) for qs, z in zip(q_sum, r_b)]
            q_sum = [qs + p[0:chunk] for qs, p in zip(q_sum, prod)]
            r_pow = [p[chunk:] for p in prod]
            yield
        p_inv = [(qs + _dot(qs.astype(BF16), bd(z.astype(BF16)))).astype(BF16) for qs, z in zip(q_sum, r_pow)]
        for i, (c, q) in enumerate(items):
            ready[c, q] = dict(ar=ar[i], akrk=akrk[i], rb=rb[i], p_inv=p_inv[i],
                               bk=jnp.concatenate([quad(prep[c]["b_h"], q), quad(prep[c]["k_h"], q)], axis=0),
                               gam_tot=quad(prep[c]["gam_tot"], q))

    def advance(c):
        rws = slice(c * chunk, (c + 1) * chunk)
        v_c = v_s[rws, :]
        v_b = v_c.astype(BF16)
        it = [ready.pop((c, q)) for q in range(n_quads)]
        first_state = (c * chunk // seq_rows) * n_quads
        s0 = [state_ref[first_state + q] for q in range(n_quads)]
        sv = [_dot_nt(it[q]["ar"], s0[q].astype(BF16)) + _dot(it[q]["akrk"], bd(quad(v_b, q)))
              for q in range(n_quads)]
        yield
        u = [_dot(it[q]["p_inv"], bd(sv[q][0:chunk].astype(BF16))) for q in range(n_quads)]
        yield
        for q in range(n_quads):
            y_s[rws, q * QUAD:(q + 1) * QUAD] = sv[q][chunk:] + _dot(it[q]["rb"], bd(u[q].astype(BF16)))
        for q in range(n_quads):
            uv_t = jnp.concatenate([u[q], quad(v_c, q)], axis=0).T.astype(BF16)
            state_ref[first_state + q] = s0[q] * it[q]["gam_tot"] + jnp.where(same_head, _dot(uv_t, it[q]["bk"]), 0.0)
        yield

    def finish(cs):
        rws = slice(cs[0] * chunk, (cs[-1] + 1) * chunk)
        y = y_s[rws, :]
        y_c = y - _head_sum(y, ones_bd) * (1.0 / HEAD)
        yield
        y_var = _head_sum(y_c * y_c, ones_bd) * (1.0 / HEAD)
        bonus = _head_sum(r_s[rws, :] * k_s[rws, :] * rk_ref[...], ones_bd) * v_s[rws, :]
        yield
        y_n = y_c * lax.rsqrt(y_var + GN_EPS) * gng_ref[...] + gnb_ref[...]
        mix = _dot(((y_n + bonus) * g_s[rws, :]).astype(BF16), wout_b[...])
        yield
        seq = rws.start // seq_rows
        local = slice(rws.start - seq * seq_rows, rws.stop - seq * seq_rows)
        o_ref[seq, local, :] = _layer_norm(ALPHA * x_ref[seq, local, :] + mix, lng_ref[...], lnb_ref[...])

    def chain(cs):
        for c in cs:
            yield from advance(c)

    per_seq = seq_rows // chunk
    groups = [[list(range(s * per_seq + g, s * per_seq + min(g + RWKV_GROUP, per_seq)))
               for g in range(0, per_seq, RWKV_GROUP)] for s in range(RWKV_SEQS)]
    n_groups = len(groups[0])
    _interleave(*[prepare(gs[0]) for gs in groups])
    for g in range(n_groups):
        stages = [chain(gs[g]) for gs in groups]
        if g + 1 < n_groups:
            stages += [prepare(gs[g + 1]) for gs in groups]
        if g > 0:
            stages += [finish(gs[g - 1]) for gs in groups]
        _interleave(*stages)
    _interleave(*[finish(gs[-1]) for gs in groups])


def _rwkv(x, consts):
    bsz, seq, d = x.shape
    rows = RWKV_ROWS
    x_spec = pl.BlockSpec((RWKV_SEQS, rows, d), lambda b, i: (b, i, 0))
    tile = pltpu.VMEM((RWKV_SEQS * rows, d), F32)
    return pl.pallas_call(
        _rwkv_kernel,
        grid=(bsz // RWKV_SEQS, seq // rows),
        in_specs=[x_spec] + [_const_spec(c.shape) for c in consts],
        out_specs=x_spec,
        out_shape=jax.ShapeDtypeStruct(x.shape, F32),
        scratch_shapes=[pltpu.VMEM(consts[j].shape, BF16) for j in RWKV_WEIGHT_SLOTS] + [
            pltpu.VMEM((SUBLANES, d), F32),
            pltpu.VMEM((RWKV_SEQS * (d // QUAD), QUAD, QUAD), F32),
            tile, tile, tile, tile, tile, tile, tile, tile,
        ],
        compiler_params=pltpu.CompilerParams(
            dimension_semantics=("arbitrary", "arbitrary"),
            vmem_limit_bytes=VMEM_LIMIT_BYTES),
        name="rwkv7",
    )(x, *consts)


def _row(p):
    return p.reshape(1, -1).astype(F32)


def kernel(x, ab_w_in, ab_b_in, conv_w, conv_b, conv_norm_g, conv_norm_b, sgu_norm_g, sgu_norm_b, sgu_w, sgu_b, ab_w_out, ab_b_out, rwkv_mu, rwkv_w_rkv, rwkv_w0, rwkv_w_w1, rwkv_w_w2, rwkv_a0, rwkv_a_w1, rwkv_a_w2, rwkv_g_w1, rwkv_g_w2, rwkv_k_k, rwkv_k_a, rwkv_r_k, rwkv_ln_g, rwkv_ln_b, rwkv_w_out, ffn_w_in, ffn_w_out, ln_mix_g, ln_mix_b, ln_ffn_g, ln_ffn_b):
    bsz, seq, d = x.shape
    for layer in range(DEPTH):
        i = layer // 2
        if layer % 2 == 0:
            sgu_b_full = jnp.broadcast_to(sgu_b[i][:, :, None], (SGU_GROUPS, SGU_CHUNK, SGU_GROUP_DIM))
            x = _mixer0(
                x, ab_w_in[i], _row(ab_b_in[i]),
                jnp.broadcast_to(conv_w[i].reshape(CONV_WIDTH, 1, CONV_DIM),
                                 (CONV_WIDTH, CONV_ROW_BLOCK, CONV_DIM)), _row(conv_b[i]),
                _row(conv_norm_g[i]), _row(conv_norm_b[i]), _row(sgu_norm_g[i]), _row(sgu_norm_b[i]),
                sgu_w[i], sgu_b_full, ab_w_out[i], _row(ab_b_out[i]),
                _row(ln_mix_g[layer]), _row(ln_mix_b[layer]))
        else:
            consts = (
                rwkv_mu[i], rwkv_w_rkv[i], _row(rwkv_w0[i]), rwkv_w_w1[i], rwkv_w_w2[i],
                _row(rwkv_a0[i]), rwkv_a_w1[i], rwkv_a_w2[i], rwkv_g_w1[i], rwkv_g_w2[i],
                _row(rwkv_k_k[i]), _row(rwkv_k_a[i]), _row(rwkv_r_k[i]), _row(rwkv_ln_g[i]),
                _row(rwkv_ln_b[i]), rwkv_w_out[i],
                _row(ln_mix_g[layer]), _row(ln_mix_b[layer]))
            x = _rwkv(x, consts)
        x = _ffn(x.reshape(bsz * seq, d), ffn_w_in, ffn_w_out, _row(ln_ffn_g[layer]), _row(ln_ffn_b[layer]),
                 layer, name=f"ffn{layer}").reshape(bsz, seq, d)
    return x
```

```python
import math

import jax
import jax.numpy as jnp
from jax import lax
from jax.experimental import pallas as pl
from jax.experimental.pallas import tpu as pltpu

F32 = jnp.float32
BF16 = jnp.bfloat16

D_MODEL = 1024
DEPTH = 2
ALPHA = (2.0 * DEPTH) ** 0.25
LN_EPS = 1e-5

CONV_DIM = 512
CONV_WIDTH = 31
SGU_DIM = 512
SGU_GROUPS = 4
SGU_GROUP_DIM = SGU_DIM // SGU_GROUPS
SGU_CHUNK = 128

HEAD = 64
GN_EPS = 64e-5
FFN_HIDDEN = 2816

LANES = 128
SUBLANES = 8
MXU_DIM = 256
VMEM_LIMIT_BYTES = 60 * 1024 * 1024

MIX0_ROWS = 512
CONV_HALO = 32
CONV_ROW_BLOCK = 32
FFN_ROWS = 512
FFN_COLS = 256
FFN_OUT_PIECES = 2
RWKV_ROWS = 256
RWKV_SEQS = 2
RWKV_CHUNK = 64
RWKV_GROUP = 2
RWKV_WEIGHT_SLOTS = (1, 3, 4, 6, 7, 8, 9, 15)
QUAD = MXU_DIM
HEADS_PER_QUAD = QUAD // HEAD


def _dot(a, b):
    return jnp.dot(a, b, preferred_element_type=F32)


def _dot_nt(a, b):
    return lax.dot_general(a, b, (((1,), (1,)), ((), ())), preferred_element_type=F32)


def _layer_norm(x, g, b, eps=LN_EPS):
    mu = jnp.mean(x, axis=-1, keepdims=True)
    xc = x - mu
    var = jnp.mean(xc * xc, axis=-1, keepdims=True)
    return xc * lax.rsqrt(var + eps) * g + b


def _sigmoid(x):
    return 1.0 / (1.0 + jnp.exp(-x))


def _gelu_tanh(x):
    c = math.sqrt(2.0 / math.pi)
    return 0.5 * x * (1.0 + jnp.tanh(c * (x + 0.044715 * (x * x * x))))


def _const_spec(shape):
    nd = len(shape)
    return pl.BlockSpec(shape, lambda *_: (0,) * nd, pipeline_mode=pl.Buffered(1))


def _mixer0_kernel(x_ref, win_ref, bin_ref, cw_ref, cb_ref, cng_ref, cnb_ref, sng_ref, snb_ref,
                   ws_ref, sb_ref, wout_ref, bout_ref, lng_ref, lnb_ref, o_ref,
                   win_b, wout_b, ybuf_ref, yrot_ref, h_ref, mixed_ref):
    rows = MIX0_ROWS
    blk_rows = SGU_CHUNK
    n_blocks = rows // blk_rows
    step = pl.program_id(1)

    @pl.when((pl.program_id(0) == 0) & (step == 0))
    def _():
        win_b[...] = win_ref[...].astype(BF16)
        wout_b[...] = wout_ref[...].astype(BF16)

    @pl.when(step == 0)
    def _():
        ybuf_ref[0:CONV_HALO, :] = jnp.zeros((CONV_HALO, CONV_DIM), F32)

    t_idx = lax.broadcasted_iota(jnp.int32, (SGU_CHUNK, SGU_CHUNK), 0)
    s_idx = lax.broadcasted_iota(jnp.int32, (SGU_CHUNK, SGU_CHUNK), 1)
    w_s = [jnp.where(t_idx >= s_idx, ws_ref[g], 0.0).astype(BF16) for g in range(SGU_GROUPS)]
    first = CONV_HALO - (CONV_WIDTH - 1)
    reach = blk_rows + CONV_HALO - SUBLANES

    def project(b):
        rws = slice(b * blk_rows, (b + 1) * blk_rows)
        h_ref[rws, :] = _dot(x_ref[0, rws, :].astype(BF16), win_b[...]) + bin_ref[...]

    def mix_block(b):
        r0 = b * blk_rows
        rws = slice(r0, r0 + blk_rows)
        ybuf_ref[CONV_HALO + r0:CONV_HALO + r0 + blk_rows, :] = (
            h_ref[rws, 0:CONV_DIM] * _sigmoid(h_ref[rws, CONV_DIM:2 * CONV_DIM]))
        lo = 0 if b == 0 else r0 + reach - blk_rows
        for r in range(1, SUBLANES):
            yrot_ref[r - 1, lo:r0 + reach, :] = ybuf_ref[lo + r:r0 + reach + r, :]
        for sub in range(blk_rows // CONV_ROW_BLOCK):
            base = r0 + sub * CONV_ROW_BLOCK
            acc = jnp.broadcast_to(cb_ref[...], (CONV_ROW_BLOCK, CONV_DIM))
            for j in range(CONV_WIDTH):
                r = (first + j) % SUBLANES
                src = ybuf_ref if r == 0 else yrot_ref.at[r - 1]
                start = base + first + j - r
                acc = acc + cw_ref[j] * src[start:start + CONV_ROW_BLOCK, :]
            ya = _layer_norm(acc, cng_ref[...], cnb_ref[...])
            mixed_ref[base:base + CONV_ROW_BLOCK, 0:CONV_DIM] = (ya * _sigmoid(ya)).astype(BF16)
        u = _gelu_tanh(h_ref[rws, 2 * CONV_DIM:2 * CONV_DIM + SGU_DIM])
        v = _layer_norm(_gelu_tanh(h_ref[rws, 2 * CONV_DIM + SGU_DIM:]), sng_ref[...], snb_ref[...]).astype(BF16)
        for g in range(SGU_GROUPS):
            cols = slice(g * SGU_GROUP_DIM, (g + 1) * SGU_GROUP_DIM)
            sv = _dot(w_s[g], v[:, cols]) + sb_ref[g]
            mixed_ref[rws, CONV_DIM + g * SGU_GROUP_DIM:CONV_DIM + (g + 1) * SGU_GROUP_DIM] = (
                u[:, cols] * sv).astype(BF16)
        mix = _dot(mixed_ref[rws, :], wout_b[...]) + bout_ref[...]
        o_ref[0, rws, :] = _layer_norm(ALPHA * x_ref[0, rws, :] + mix, lng_ref[...], lnb_ref[...])

    project(0)
    for b in range(n_blocks):
        if b + 1 < n_blocks:
            project(b + 1)
        mix_block(b)
    ybuf_ref[0:CONV_HALO, :] = ybuf_ref[rows:rows + CONV_HALO, :]


def _mixer0(x, w_in, b_in, conv_w, conv_b, cn_g, cn_b, sn_g, sn_b, sgu_w, sgu_b_full,
            w_out, b_out, ln_g, ln_b):
    bsz, seq, d = x.shape
    rows = MIX0_ROWS
    x_spec = pl.BlockSpec((1, rows, d), lambda b, i: (b, i, 0))
    consts = (w_in, b_in, conv_w, conv_b, cn_g, cn_b, sn_g, sn_b, sgu_w, sgu_b_full,
              w_out, b_out, ln_g, ln_b)
    return pl.pallas_call(
        _mixer0_kernel,
        grid=(bsz, seq // rows),
        in_specs=[x_spec] + [_const_spec(c.shape) for c in consts],
        out_specs=x_spec,
        out_shape=jax.ShapeDtypeStruct(x.shape, F32),
        scratch_shapes=[
            pltpu.VMEM(w_in.shape, BF16),
            pltpu.VMEM(w_out.shape, BF16),
            pltpu.VMEM((CONV_HALO + rows, CONV_DIM), F32),
            pltpu.VMEM((SUBLANES - 1, CONV_HALO + rows - SUBLANES, CONV_DIM), F32),
            pltpu.VMEM((rows, 2 * CONV_DIM + 2 * SGU_DIM), F32),
            pltpu.VMEM((rows, CONV_DIM + SGU_DIM), BF16),
        ],
        compiler_params=pltpu.CompilerParams(
            dimension_semantics=("arbitrary", "arbitrary"),
            vmem_limit_bytes=VMEM_LIMIT_BYTES),
        name="mixer0",
    )(x, *consts)


def _ffn_kernel(x_ref, win_ref, wout_ref, g_ref, b_ref, o_ref, act_ref):
    x = x_ref[...]
    xb = x.astype(BF16)
    for c in range(FFN_HIDDEN // FFN_COLS):
        lo = c * FFN_COLS
        gate = _dot(xb, win_ref[0, :, lo:lo + FFN_COLS].astype(BF16))
        up = _dot(xb, win_ref[0, :, FFN_HIDDEN + lo:FFN_HIDDEN + lo + FFN_COLS].astype(BF16))
        act_ref[:, lo:lo + FFN_COLS] = (gate * _sigmoid(gate) * up).astype(BF16)
    w_out = wout_ref[0].astype(BF16)
    piece = FFN_ROWS // FFN_OUT_PIECES
    for p in range(FFN_OUT_PIECES):
        rws = slice(p * piece, (p + 1) * piece)
        y = _dot(act_ref[rws, :], w_out)
        o_ref[rws, :] = _layer_norm(ALPHA * x_ref[rws, :] + y, g_ref[...], b_ref[...])


def _ffn(x2d, w_in_all, w_out_all, ln_g, ln_b, layer, name):
    n, d = x2d.shape
    rows = FFN_ROWS
    x_spec = pl.BlockSpec((rows, d), lambda i: (i, 0))
    layer_spec = lambda w: pl.BlockSpec((1,) + w.shape[1:], lambda i: (layer, 0, 0),
                                        pipeline_mode=pl.Buffered(1))
    return pl.pallas_call(
        _ffn_kernel,
        grid=(n // rows,),
        in_specs=[x_spec, layer_spec(w_in_all), layer_spec(w_out_all),
                  _const_spec(ln_g.shape), _const_spec(ln_b.shape)],
        out_specs=x_spec,
        out_shape=jax.ShapeDtypeStruct(x2d.shape, F32),
        scratch_shapes=[pltpu.VMEM((rows, FFN_HIDDEN), BF16)],
        compiler_params=pltpu.CompilerParams(
            dimension_semantics=("arbitrary",),
            vmem_limit_bytes=VMEM_LIMIT_BYTES),
        name=name,
    )(x2d, w_in_all, w_out_all, ln_g, ln_b)


def _block_diag(z, head_lanes):
    return jnp.concatenate([z * head_lanes[h] for h in range(HEADS_PER_QUAD)], axis=0)


def _interleave(*stages):
    live = list(stages)
    while live:
        for stage in list(live):
            try:
                next(stage)
            except StopIteration:
                live.remove(stage)


def _head_sum(z, ones_bd):
    n, width = z.shape
    zb = z.astype(BF16)
    stacked = jnp.concatenate([zb[:, q * QUAD:(q + 1) * QUAD] for q in range(width // QUAD)], axis=0)
    sums = _dot(stacked, ones_bd)
    return jnp.concatenate([sums[q * n:(q + 1) * n] for q in range(width // QUAD)], axis=1)


def _rwkv_kernel(x_ref, mu_ref, wrkv_ref, w0_ref, ww1_ref, ww2_ref, a0_ref,
                 aw1_ref, aw2_ref, gw1_ref, gw2_ref, kk_ref, ka_ref, rk_ref, gng_ref, gnb_ref,
                 wout_ref, lng_ref, lnb_ref, o_ref,
                 wrkv_b, ww1_b, ww2_b, aw1_b, aw2_b, gw1_b, gw2_b, wout_b,
                 xlast_ref, state_ref, r_s, k_s, v_s, kk_s, kka_s, lw_s, g_s, y_s):
    seq_rows = RWKV_ROWS
    rows = RWKV_SEQS * seq_rows
    chunk = RWKV_CHUNK
    d = D_MODEL
    n_quads = d // QUAD
    step = pl.program_id(1)

    @pl.when((pl.program_id(0) == 0) & (step == 0))
    def _():
        for j in range(3):
            wrkv_b[j] = wrkv_ref[j].astype(BF16)
        for src, dst in ((ww1_ref, ww1_b), (ww2_ref, ww2_b), (aw1_ref, aw1_b), (aw2_ref, aw2_b),
                         (gw1_ref, gw1_b), (gw2_ref, gw2_b), (wout_ref, wout_b)):
            dst[...] = src[...].astype(BF16)

    @pl.when(step == 0)
    def _():
        xlast_ref[...] = jnp.zeros(xlast_ref.shape, F32)
        state_ref[...] = jnp.zeros(state_ref.shape, F32)

    x = jnp.concatenate([x_ref[s] for s in range(RWKV_SEQS)], axis=0)
    row_id = lax.broadcasted_iota(jnp.int32, (rows, d), 0)
    x_prev = pltpu.roll(x, 1, axis=0)
    for s in range(RWKV_SEQS):
        x_prev = jnp.where(row_id == s * seq_rows, xlast_ref[s:s + 1, :], x_prev)
        xlast_ref[s:s + 1, :] = x[(s + 1) * seq_rows - 1:(s + 1) * seq_rows, :]
    xx = x_prev - x

    def mixed(j):
        return (x + xx * mu_ref[j:j + 1, :]).astype(BF16)

    r_i = lax.broadcasted_iota(jnp.int32, (QUAD, QUAD), 0) // HEAD
    c_i = lax.broadcasted_iota(jnp.int32, (QUAD, QUAD), 1) // HEAD
    same_head = r_i == c_i
    ones_bd = jnp.where(same_head, 1.0, 0.0).astype(BF16)

    r = _dot(mixed(0), wrkv_b[0])
    k = _dot(mixed(2), wrkv_b[1])
    v = _dot(mixed(3), wrkv_b[2])
    z_w = w0_ref[...] + _dot(jnp.tanh(_dot(mixed(1), ww1_b[...])).astype(BF16), ww2_b[...])
    lw_s[...] = -math.exp(-0.5) * _sigmoid(z_w)
    a = _sigmoid(a0_ref[...] + _dot(_dot(mixed(4), aw1_b[...]).astype(BF16), aw2_b[...]))
    kk = k * kk_ref[...]
    kk = kk / jnp.maximum(jnp.sqrt(_head_sum(kk * kk, ones_bd)), 1e-12)
    k2 = k * (1.0 + (a - 1.0) * ka_ref[...])
    g_s[...] = _dot(_sigmoid(_dot(mixed(5), gw1_b[...])).astype(BF16), gw2_b[...])
    r_s[...] = r
    k_s[...] = k2
    v_s[...] = v
    kk_s[...] = kk
    kka_s[...] = kk * a

    t_i = lax.broadcasted_iota(jnp.int32, (chunk, chunk), 0)
    s_i = lax.broadcasted_iota(jnp.int32, (chunk, chunk), 1)
    tri_incl = jnp.where(t_i >= s_i, 1.0, 0.0).astype(BF16)
    tri_pair = jnp.concatenate([tri_incl, tri_incl], axis=1)
    t_n = lax.broadcasted_iota(jnp.int32, (2 * chunk, QUAD), 0)
    s_n = lax.broadcasted_iota(jnp.int32, (2 * chunk, QUAD), 1) % HEAD
    causal2 = jnp.where(t_n < chunk, t_n, t_n - chunk + 1) > s_n
    eye_nat = jnp.where(lax.broadcasted_iota(jnp.int32, (chunk, QUAD), 0)
                        == lax.broadcasted_iota(jnp.int32, (chunk, QUAD), 1) % HEAD, 1.0, 0.0).astype(F32)

    n_chunks = rows // chunk
    quad = lambda z, q: z[:, q * QUAD:(q + 1) * QUAD]
    lane_head = lax.broadcasted_iota(jnp.int32, (1, QUAD), 1) // HEAD
    head_lanes = [jnp.where(lane_head == h, 1.0, 0.0).astype(BF16) for h in range(HEADS_PER_QUAD)]
    bd = lambda z: _block_diag(z, head_lanes)
    ready = {}

    def prepare(cs):
        items = [(c, q) for c in cs for q in range(n_quads)]
        prep = {}
        for c in cs:
            rws = slice(c * chunk, (c + 1) * chunk)
            lw = lw_s[rws, :]
            lw_hi = lw.astype(BF16)
            lw_lo = (lw - lw_hi.astype(F32)).astype(BF16)
            cum = _dot(tri_pair, jnp.concatenate([lw_hi, lw_lo], axis=0))
            tot = cum[chunk - 1:chunk, :]
            gam_inv = jnp.exp(-cum)
            gam_rest = jnp.exp(tot - cum)
            kk_c = kk_s[rws, :]
            kka_c = kka_s[rws, :]
            k_c = k_s[rws, :]
            prep[c] = dict(
                a_t=(-kk_c * jnp.exp(cum - lw)).astype(BF16),
                r_t=(r_s[rws, :] * jnp.exp(cum)).astype(BF16),
                b_t=(kka_c * gam_inv).astype(BF16),
                k_t=(k_c * gam_inv).astype(BF16),
                b_h=(kka_c * gam_rest).astype(BF16),
                k_h=(k_c * gam_rest).astype(BF16),
                gam_tot=jnp.exp(tot))
        yield
        ar = [jnp.concatenate([quad(prep[c]["a_t"], q), quad(prep[c]["r_t"], q)], axis=0) for c, q in items]
        s_b = [jnp.where(causal2, _dot_nt(ar[i], bd(quad(prep[c]["b_t"], q))), 0.0)
               for i, (c, q) in enumerate(items)]
        akrk = [jnp.where(causal2, _dot_nt(ar[i], bd(quad(prep[c]["k_t"], q))), 0.0).astype(BF16)
                for i, (c, q) in enumerate(items)]
        rb = [z[chunk:].astype(BF16) for z in s_b]
        yield
        r_pow = [z[0:chunk] for z in s_b]
        q_sum = [eye_nat + z for z in r_pow]
        r_pow = [_dot(z.astype(BF16), bd(z.astype(BF16))) for z in r_pow]
        yield
        for _ in range(int(math.log2(chunk)) - 2):
            r_b = [z.astype(BF16) for z in r_pow]
            prod = [_dot(jnp.concatenate([qs.astype(BF16), z], axis=0), bd(z)) for qs, z in zip(q_sum, r_b)]
            q_sum = [qs + p[0:chunk] for qs, p in zip(q_sum, prod)]
            r_pow = [p[chunk:] for p in prod]
            yield
        p_inv = [(qs + _dot(qs.astype(BF16), bd(z.astype(BF16)))).astype(BF16) for qs, z in zip(q_sum, r_pow)]
        for i, (c, q) in enumerate(items):
            ready[c, q] = dict(ar=ar[i], akrk=akrk[i], rb=rb[i], p_inv=p_inv[i],
                               bk=jnp.concatenate([quad(prep[c]["b_h"], q), quad(prep[c]["k_h"], q)], axis=0),
                               gam_tot=quad(prep[c]["gam_tot"], q))

    def advance(c):
        rws = slice(c * chunk, (c + 1) * chunk)
        v_c = v_s[rws, :]
        v_b = v_c.astype(BF16)
        it = [ready.pop((c, q)) for q in range(n_quads)]
        first_state = (c * chunk // seq_rows) * n_quads
        s0 = [state_ref[first_state + q] for q in range(n_quads)]
        sv = [_dot_nt(it[q]["ar"], s0[q].astype(BF16)) + _dot(it[q]["akrk"], bd(quad(v_b, q)))
              for q in range(n_quads)]
        yield
        u = [_dot(it[q]["p_inv"], bd(sv[q][0:chunk].astype(BF16))) for q in range(n_quads)]
        yield
        for q in range(n_quads):
            y_s[rws, q * QUAD:(q + 1) * QUAD] = sv[q][chunk:] + _dot(it[q]["rb"], bd(u[q].astype(BF16)))
        for q in range(n_quads):
            uv_t = jnp.concatenate([u[q], quad(v_c, q)], axis=0).T.astype(BF16)
            state_ref[first_state + q] = s0[q] * it[q]["gam_tot"] + jnp.where(same_head, _dot(uv_t, it[q]["bk"]), 0.0)
        yield

    def finish(cs):
        rws = slice(cs[0] * chunk, (cs[-1] + 1) * chunk)
        y = y_s[rws, :]
        y_c = y - _head_sum(y, ones_bd) * (1.0 / HEAD)
        yield
        y_var = _head_sum(y_c * y_c, ones_bd) * (1.0 / HEAD)
        bonus = _head_sum(r_s[rws, :] * k_s[rws, :] * rk_ref[...], ones_bd) * v_s[rws, :]
        yield
        y_n = y_c * lax.rsqrt(y_var + GN_EPS) * gng_ref[...] + gnb_ref[...]
        mix = _dot(((y_n + bonus) * g_s[rws, :]).astype(BF16), wout_b[...])
        yield
        seq = rws.start // seq_rows
        local = slice(rws.start - seq * seq_rows, rws.stop - seq * seq_rows)
        o_ref[seq, local, :] = _layer_norm(ALPHA * x_ref[seq, local, :] + mix, lng_ref[...], lnb_ref[...])

    def chain(cs):
        for c in cs:
            yield from advance(c)

    per_seq = seq_rows // chunk
    groups = [[list(range(s * per_seq + g, s * per_seq + min(g + RWKV_GROUP, per_seq)))
               for g in range(0, per_seq, RWKV_GROUP)] for s in range(RWKV_SEQS)]
    n_groups = len(groups[0])
    _interleave(*[prepare(gs[0]) for gs in groups])
    for g in range(n_groups):
        stages = [chain(gs[g]) for gs in groups]
        if g + 1 < n_groups:
            stages += [prepare(gs[g + 1]) for gs in groups]
        if g > 0:
            stages += [finish(gs[g - 1]) for gs in groups]
        _interleave(*stages)
    _interleave(*[finish(gs[-1]) for gs in groups])


def _rwkv(x, consts):
    bsz, seq, d = x.shape
    rows = RWKV_ROWS
    x_spec = pl.BlockSpec((RWKV_SEQS, rows, d), lambda b, i: (b, i, 0))
    tile = pltpu.VMEM((RWKV_SEQS * rows, d), F32)
    return pl.pallas_call(
        _rwkv_kernel,
        grid=(bsz // RWKV_SEQS, seq // rows),
        in_specs=[x_spec] + [_const_spec(c.shape) for c in consts],
        out_specs=x_spec,
        out_shape=jax.ShapeDtypeStruct(x.shape, F32),
        scratch_shapes=[pltpu.VMEM(consts[j].shape, BF16) for j in RWKV_WEIGHT_SLOTS] + [
            pltpu.VMEM((SUBLANES, d), F32),
            pltpu.VMEM((RWKV_SEQS * (d // QUAD), QUAD, QUAD), F32),
            tile, tile, tile, tile, tile, tile, tile, tile,
        ],
        compiler_params=pltpu.CompilerParams(
            dimension_semantics=("arbitrary", "arbitrary"),
            vmem_limit_bytes=VMEM_LIMIT_BYTES),
        name="rwkv7",
    )(x, *consts)


def _row(p):
    return p.reshape(1, -1).astype(F32)


def kernel(x, ab_w_in, ab_b_in, conv_w, conv_b, conv_norm_g, conv_norm_b, sgu_norm_g, sgu_norm_b, sgu_w, sgu_b, ab_w_out, ab_b_out, rwkv_mu, rwkv_w_rkv, rwkv_w0, rwkv_w_w1, rwkv_w_w2, rwkv_a0, rwkv_a_w1, rwkv_a_w2, rwkv_g_w1, rwkv_g_w2, rwkv_k_k, rwkv_k_a, rwkv_r_k, rwkv_ln_g, rwkv_ln_b, rwkv_w_out, ffn_w_in, ffn_w_out, ln_mix_g, ln_mix_b, ln_ffn_g, ln_ffn_b):
    bsz, seq, d = x.shape
    for layer in range(DEPTH):
        i = layer // 2
        if layer % 2 == 0:
            sgu_b_full = jnp.broadcast_to(sgu_b[i][:, :, None], (SGU_GROUPS, SGU_CHUNK, SGU_GROUP_DIM))
            x = _mixer0(
                x, ab_w_in[i], _row(ab_b_in[i]),
                jnp.broadcast_to(conv_w[i].reshape(CONV_WIDTH, 1, CONV_DIM),
                                 (CONV_WIDTH, CONV_ROW_BLOCK, CONV_DIM)), _row(conv_b[i]),
                _row(conv_norm_g[i]), _row(conv_norm_b[i]), _row(sgu_norm_g[i]), _row(sgu_norm_b[i]),
                sgu_w[i], sgu_b_full, ab_w_out[i], _row(ab_b_out[i]),
                _row(ln_mix_g[layer]), _row(ln_mix_b[layer]))
        else:
            consts = (
                rwkv_mu[i], rwkv_w_rkv[i], _row(rwkv_w0[i]), rwkv_w_w1[i], rwkv_w_w2[i],
                _row(rwkv_a0[i]), rwkv_a_w1[i], rwkv_a_w2[i], rwkv_g_w1[i], rwkv_g_w2[i],
                _row(rwkv_k_k[i]), _row(rwkv_k_a[i]), _row(rwkv_r_k[i]), _row(rwkv_ln_g[i]),
                _row(rwkv_ln_b[i]), rwkv_w_out[i],
                _row(ln_mix_g[layer]), _row(ln_mix_b[layer]))
            x = _rwkv(x, consts)
        x = _ffn(x.reshape(bsz * seq, d), ffn_w_in, ffn_w_out, _row(ln_ffn_g[layer]), _row(ln_ffn_b[layer]),
                 layer, name=f"ffn{layer}").reshape(bsz, seq, d)
    return x
```
